```python
import math
import jax, jax.numpy as jnp
from jax import lax
import numpy as np

D_MODEL = 1024
BATCH = 2
SEQ = 8192
DEPTH = 4
DEC_BATCH = 32
DEC_SEQ = 1
PAST_LEN = 8192
PAGE_SIZE = 128

SSM_WIDTH = D_MODEL // 4
SSM_GROUP = 16
SSM_GROUPS = SSM_WIDTH // SSM_GROUP
SSM_STATE = 64
DT_MIN = 0.001
DT_MAX = 0.1
CHUNK = 128
SGU_WIDTH = D_MODEL // 4
SGU_GROUPS = 4
SGU_GROUP_DIM = SGU_WIDTH // SGU_GROUPS
HEAD_DIM = 64
N_HEADS = (D_MODEL // 2) // HEAD_DIM
ATTN_WIDTH = N_HEADS * HEAD_DIM
MOBA_BLOCK = 256
MOBA_TOPK = 3
Q_BLOCK = 128
ROPE_THETA = 10000.0
N_BRANCH = 3
D_FF = 4 * D_MODEL
IN_WIDTH = SSM_WIDTH + 2 * SGU_WIDTH + 3 * ATTN_WIDTH + N_BRANCH * D_MODEL
EPS = 1e-6
NEG = -1e30

kernel_name = 'hybrid_s5_sgu_moba_decoder_step'


def rmsnorm(x, g):
    xf = x.astype(jnp.float32)
    y = xf * lax.rsqrt(jnp.mean(xf * xf, axis=-1, keepdims=True) + EPS)
    return (y * g.astype(jnp.float32)).astype(x.dtype)


def rope(x, pos):
    half = HEAD_DIM // 2
    inv_freq = ROPE_THETA ** (-jnp.arange(half, dtype=jnp.float32) / half)
    ang = pos.astype(jnp.float32)[:, None] * inv_freq[None, :]
    cos = jnp.cos(ang)[None, :, None, :]
    sin = jnp.sin(ang)[None, :, None, :]
    xf = x.astype(jnp.float32)
    x1, x2 = xf[..., :half], xf[..., half:]
    return jnp.concatenate([x1 * cos - x2 * sin, x2 * cos + x1 * sin], axis=-1).astype(x.dtype)


def ssm_mixer(u, h0_re, h0_im, a_re, a_im, log_dt, b_re, b_im, c_re, c_im, d_skip, w_glu, b_glu):
    f32 = jnp.float32
    bsz, t = u.shape[0], u.shape[1]
    uf = u.astype(f32)
    ug = uf.reshape(bsz, t, SSM_GROUPS, SSM_GROUP)
    lam_re = jnp.minimum(a_re.astype(f32), -1e-4)
    lam_im = a_im.astype(f32)
    dt = jnp.exp(log_dt.astype(f32))[:, None]
    mag = jnp.exp(lam_re * dt)
    ab_re = mag * jnp.cos(lam_im * dt)
    ab_im = mag * jnp.sin(lam_im * dt)
    den = lam_re * lam_re + lam_im * lam_im
    num_re = ab_re - 1.0
    f_re = (num_re * lam_re + ab_im * lam_im) / den
    f_im = (ab_im * lam_re - num_re * lam_im) / den
    br, bi = b_re.astype(f32), b_im.astype(f32)
    bb_re = f_re[..., None] * br - f_im[..., None] * bi
    bb_im = f_re[..., None] * bi + f_im[..., None] * br
    bu_re = jnp.einsum('btgh,gph->btgp', ug, bb_re)
    bu_im = jnp.einsum('btgh,gph->btgp', ug, bb_im)
    h0r, h0i = h0_re.astype(f32), h0_im.astype(f32)
    bu_re = bu_re.at[:, 0].add(ab_re * h0r - ab_im * h0i)
    bu_im = bu_im.at[:, 0].add(ab_re * h0i + ab_im * h0r)
    a_re_t = jnp.broadcast_to(ab_re, bu_re.shape)
    a_im_t = jnp.broadcast_to(ab_im, bu_im.shape)

    def combine(e1, e2):
        a1r, a1i, b1r, b1i = e1
        a2r, a2i, b2r, b2i = e2
        return (a2r * a1r - a2i * a1i, a2r * a1i + a2i * a1r,
                a2r * b1r - a2i * b1i + b2r, a2r * b1i + a2i * b1r + b2i)

    _, _, h_re, h_im = lax.associative_scan(combine, (a_re_t, a_im_t, bu_re, bu_im), axis=1)
    y = (jnp.einsum('btgp,ghp->btgh', h_re, c_re.astype(f32))
         - jnp.einsum('btgp,ghp->btgh', h_im, c_im.astype(f32)))
    y = y.reshape(bsz, t, SSM_WIDTH) + d_skip.astype(f32) * uf
    g = jax.nn.gelu(y)
    out = g * jax.nn.sigmoid(g @ w_glu.astype(f32) + b_glu.astype(f32))
    return out.astype(u.dtype), h_re[:, -1], h_im[:, -1]


def chunk_sgu(u, v, w_s, b_s, ln_g, ln_b):
    f32 = jnp.float32
    bsz, t = u.shape[0], u.shape[1]
    tp = -(-t // CHUNK) * CHUNK
    vf = v.astype(f32)
    mu = jnp.mean(vf, axis=-1, keepdims=True)
    var = jnp.mean(jnp.square(vf - mu), axis=-1, keepdims=True)
    vn = (vf - mu) * lax.rsqrt(var + EPS) * ln_g.astype(f32) + ln_b.astype(f32)
    vp = jnp.pad(vn, ((0, 0), (0, tp - t), (0, 0)))
    vp = vp.reshape(bsz, tp // CHUNK, CHUNK, SGU_GROUPS, SGU_GROUP_DIM)
    mask = jnp.tril(jnp.ones((CHUNK, CHUNK), dtype=bool))
    ws = jnp.where(mask[None], w_s.astype(f32), 0.0)
    mixed = jnp.einsum('gij,bcjgd->bcigd', ws, vp) + b_s.astype(f32).T[None, None, :, :, None]
    mixed = mixed.reshape(bsz, tp, SGU_WIDTH)[:, :t]
    return (u.astype(f32) * mixed).astype(u.dtype), vn.astype(v.dtype)


def moba_attention(q, k, v, q_pos):
    f32 = jnp.float32
    bsz, L = k.shape[0], k.shape[1]
    tq = q.shape[1]
    nb = -(-L // MOBA_BLOCK)
    lp = nb * MOBA_BLOCK
    pad = ((0, 0), (0, lp - L), (0, 0), (0, 0))
    kb = jnp.pad(k, pad).reshape(bsz, nb, MOBA_BLOCK, N_HEADS, HEAD_DIM).transpose(0, 3, 1, 2, 4)
    vb = jnp.pad(v, pad).reshape(bsz, nb, MOBA_BLOCK, N_HEADS, HEAD_DIM).transpose(0, 3, 1, 2, 4)
    kmean = jnp.mean(kb.astype(f32), axis=3)
    n_sel = min(MOBA_TOPK, nb)
    qb = math.gcd(tq, Q_BLOCK)
    n_qb = tq // qb
    qh = q.reshape(bsz, n_qb, qb, N_HEADS, HEAD_DIM).transpose(1, 0, 3, 2, 4)
    posb = q_pos.reshape(n_qb, qb)
    b_idx = jnp.arange(bsz)[:, None, None, None]
    h_idx = jnp.arange(N_HEADS)[None, :, None, None]
    scale = HEAD_DIM ** -0.5

    def one_block(args):
        qc, pc = args
        qf = qc.astype(f32)
        own = pc // MOBA_BLOCK
        gate = jnp.einsum('bhqd,bhnd->bhqn', qf, kmean)
        fully_past = jnp.arange(nb)[None, :] < own[:, None]
        gate = jnp.where(fully_past, gate, NEG)
        _, sel = lax.top_k(gate, n_sel)
        valid = sel < own[:, None]
        own_b = jnp.broadcast_to(own[:, None], sel.shape[:-1] + (1,))
        idx = jnp.concatenate([sel, own_b], axis=-1)
        ok = jnp.concatenate([valid, jnp.ones(own_b.shape, dtype=bool)], axis=-1)
        kg = kb[b_idx, h_idx, idx].astype(f32)
        vg = vb[b_idx, h_idx, idx].astype(f32)
        kpos = idx[..., None] * MOBA_BLOCK + jnp.arange(MOBA_BLOCK)
        allowed = ok[..., None] & (kpos <= pc[:, None, None])
        s = jnp.einsum('bhqd,bhqnsd->bhqns', qf, kg) * scale
        s = jnp.where(allowed, s, NEG)
        p = jax.nn.softmax(s.reshape(bsz, N_HEADS, qb, -1), axis=-1).reshape(s.shape)
        o = jnp.einsum('bhqns,bhqnsd->bhqd', p, vg)
        return o.astype(q.dtype)

    out = lax.map(one_block, (qh, posb))
    return out.transpose(1, 0, 3, 2, 4).reshape(bsz, tq, ATTN_WIDTH)


def trunk_layer(x, pos, h0_re, h0_im, past_k, past_v, lp):
    f32 = jnp.float32
    bsz, t = x.shape[0], x.shape[1]
    h = rmsnorm(x, lp['mix_pre_g'])
    z = h @ lp['w_in']
    offs = np.cumsum([SSM_WIDTH, SGU_WIDTH, SGU_WIDTH, ATTN_WIDTH, ATTN_WIDTH, ATTN_WIDTH]).tolist()
    u_ssm, u_sgu, v_sgu, q, k, v, gates = jnp.split(z, offs, axis=-1)
    y_a, hT_re, hT_im = ssm_mixer(u_ssm, h0_re, h0_im, lp['ssm_a_re'], lp['ssm_a_im'], lp['ssm_log_dt'],
                                  lp['ssm_b_re'], lp['ssm_b_im'], lp['ssm_c_re'], lp['ssm_c_im'],
                                  lp['ssm_d'], lp['ssm_w_glu'], lp['ssm_b_glu'])
    y_b, v_rows = chunk_sgu(u_sgu, v_sgu, lp['sgu_w'], lp['sgu_b'], lp['sgu_ln_g'], lp['sgu_ln_b'])
    q = rope(q.reshape(bsz, t, N_HEADS, HEAD_DIM), pos)
    k = rope(k.reshape(bsz, t, N_HEADS, HEAD_DIM), pos)
    v = v.reshape(bsz, t, N_HEADS, HEAD_DIM)
    k_all = k if past_k is None else jnp.concatenate([past_k.astype(k.dtype), k], axis=1)
    v_all = v if past_v is None else jnp.concatenate([past_v.astype(v.dtype), v], axis=1)
    y_c = moba_attention(q, k_all, v_all, pos)
    g = jax.nn.sigmoid(gates.astype(f32)).reshape(bsz, t, N_BRANCH, D_MODEL)
    merged = (g[:, :, 0] * (y_a @ lp['w_out_a']).astype(f32)
              + g[:, :, 1] * (y_b @ lp['w_out_b']).astype(f32)
              + g[:, :, 2] * (y_c @ lp['w_out_c']).astype(f32))
    mix = merged.astype(x.dtype) @ lp['w_o']
    x = x + rmsnorm(mix, lp['mix_post_g'])
    m = rmsnorm(x, lp['ffn_pre_g']) @ lp['w_up']
    m = jnp.square(jax.nn.relu(m)) @ lp['w_down']
    x = x + rmsnorm(m, lp['ffn_post_g'])
    return x, k, v, hT_re, hT_im, v_rows


def setup_inputs(seed: int = 0) -> dict:
    key = jax.random.key(seed)
    ks = jax.random.split(key, 40)
    f32 = jnp.float32

    def nrm(k, shape, scale):
        return jax.random.normal(k, shape, f32) * scale

    n_pages = PAST_LEN // PAGE_SIZE
    n_used = DEC_BATCH * n_pages
    n_pool = n_used + (n_used + 3) // 4
    perm = jax.random.permutation(ks[0], n_pool)
    page_table = perm[:n_used].reshape(DEC_BATCH, n_pages).astype(jnp.int32)

    rows = jnp.arange(CHUNK, dtype=f32)[:, None] + 1.0
    a_im0 = jnp.pi * jnp.arange(SSM_STATE, dtype=f32)
    return {
        'x_prompt': nrm(ks[1], (BATCH, SEQ, D_MODEL), 1.0),
        'x_sample': nrm(ks[2], (DEC_BATCH, DEC_SEQ, D_MODEL), 1.0),
        'cache_k': nrm(ks[3], (DEPTH, n_pool, PAGE_SIZE, N_HEADS, HEAD_DIM), 1.0),
        'cache_v': nrm(ks[4], (DEPTH, n_pool, PAGE_SIZE, N_HEADS, HEAD_DIM), 1.0),
        'state_ssm_re': nrm(ks[5], (DEPTH, DEC_BATCH, SSM_GROUPS, SSM_STATE), 0.3),
        'state_ssm_im': nrm(ks[6], (DEPTH, DEC_BATCH, SSM_GROUPS, SSM_STATE), 0.3),
        'page_table': page_table,
        'mix_pre_g': 1.0 + nrm(ks[7], (DEPTH, D_MODEL), 0.02),
        'w_in': nrm(ks[8], (DEPTH, D_MODEL, IN_WIDTH), D_MODEL ** -0.5),
        'ssm_a_re': -0.5 + nrm(ks[9], (DEPTH, SSM_GROUPS, SSM_STATE), 0.01),
        'ssm_a_im': a_im0 + nrm(ks[10], (DEPTH, SSM_GROUPS, SSM_STATE), 0.01),
        'ssm_log_dt': jax.random.uniform(ks[11], (DEPTH, SSM_GROUPS), f32, math.log(DT_MIN), math.log(DT_MAX)),
        'ssm_b_re': nrm(ks[12], (DEPTH, SSM_GROUPS, SSM_STATE, SSM_GROUP), (2 * SSM_GROUP) ** -0.5),
        'ssm_b_im': nrm(ks[13], (DEPTH, SSM_GROUPS, SSM_STATE, SSM_GROUP), (2 * SSM_GROUP) ** -0.5),
        'ssm_c_re': nrm(ks[14], (DEPTH, SSM_GROUPS, SSM_GROUP, SSM_STATE), SSM_STATE ** -0.5),
        'ssm_c_im': nrm(ks[15], (DEPTH, SSM_GROUPS, SSM_GROUP, SSM_STATE), SSM_STATE ** -0.5),
        'ssm_d': nrm(ks[16], (DEPTH, SSM_WIDTH), 1.0),
        'ssm_w_glu': nrm(ks[17], (DEPTH, SSM_WIDTH, SSM_WIDTH), SSM_WIDTH ** -0.5),
        'ssm_b_glu': nrm(ks[18], (DEPTH, SSM_WIDTH), 0.02),
        'sgu_ln_g': 1.0 + nrm(ks[19], (DEPTH, SGU_WIDTH), 0.02),
        'sgu_ln_b': nrm(ks[20], (DEPTH, SGU_WIDTH), 0.02),
        'sgu_w': nrm(ks[21], (DEPTH, SGU_GROUPS, CHUNK, CHUNK), 1.0) / jnp.sqrt(rows),
        'sgu_b': nrm(ks[22], (DEPTH, SGU_GROUPS, CHUNK), 0.02),
        'w_out_a': nrm(ks[23], (DEPTH, SSM_WIDTH, D_MODEL), SSM_WIDTH ** -0.5),
        'w_out_b': nrm(ks[24], (DEPTH, SGU_WIDTH, D_MODEL), SGU_WIDTH ** -0.5),
        'w_out_c': nrm(ks[25], (DEPTH, ATTN_WIDTH, D_MODEL), ATTN_WIDTH ** -0.5),
        'w_o': nrm(ks[26], (DEPTH, D_MODEL, D_MODEL), D_MODEL ** -0.5),
        'mix_post_g': 1.0 + nrm(ks[27], (DEPTH, D_MODEL), 0.02),
        'ffn_pre_g': 1.0 + nrm(ks[28], (DEPTH, D_MODEL), 0.02),
        'w_up': nrm(ks[29], (DEPTH, D_MODEL, D_FF), D_MODEL ** -0.5),
        'w_down': nrm(ks[30], (DEPTH, D_FF, D_MODEL), D_FF ** -0.5),
        'ffn_post_g': 1.0 + nrm(ks[31], (DEPTH, D_MODEL), 0.02),
    }


def reference(x_prompt, x_sample, cache_k, cache_v, state_ssm_re, state_ssm_im, page_table,
              mix_pre_g, w_in, ssm_a_re, ssm_a_im, ssm_log_dt, ssm_b_re, ssm_b_im, ssm_c_re, ssm_c_im,
              ssm_d, ssm_w_glu, ssm_b_glu, sgu_ln_g, sgu_ln_b, sgu_w, sgu_b,
              w_out_a, w_out_b, w_out_c, w_o, mix_post_g, ffn_pre_g, w_up, w_down, ffn_post_g):
    n_pages = page_table.shape[1]
    past_len = n_pages * PAGE_SIZE
    bp, tp = x_prompt.shape[0], x_prompt.shape[1]
    bs, ts = x_sample.shape[0], x_sample.shape[1]
    pos_p = jnp.arange(tp, dtype=jnp.int32)
    pos_s = past_len + jnp.arange(ts, dtype=jnp.int32)
    h0_zero = jnp.zeros((bp, SSM_GROUPS, SSM_STATE), jnp.float32)
    xp, xs = x_prompt, x_sample
    kp_l, vp_l, hrp_l, hip_l = [], [], [], []
    ks_l, vs_l, hrs_l, his_l, sv_l = [], [], [], [], []
    for l in range(DEPTH):
        lp = {
            'mix_pre_g': mix_pre_g[l], 'w_in': w_in[l],
            'ssm_a_re': ssm_a_re[l], 'ssm_a_im': ssm_a_im[l], 'ssm_log_dt': ssm_log_dt[l],
            'ssm_b_re': ssm_b_re[l], 'ssm_b_im': ssm_b_im[l], 'ssm_c_re': ssm_c_re[l], 'ssm_c_im': ssm_c_im[l],
            'ssm_d': ssm_d[l], 'ssm_w_glu': ssm_w_glu[l], 'ssm_b_glu': ssm_b_glu[l],
            'sgu_ln_g': sgu_ln_g[l], 'sgu_ln_b': sgu_ln_b[l], 'sgu_w': sgu_w[l], 'sgu_b': sgu_b[l],
            'w_out_a': w_out_a[l], 'w_out_b': w_out_b[l], 'w_out_c': w_out_c[l], 'w_o': w_o[l],
            'mix_post_g': mix_post_g[l], 'ffn_pre_g': ffn_pre_g[l], 'w_up': w_up[l], 'w_down': w_down[l],
            'ffn_post_g': ffn_post_g[l],
        }
        xp, k_p, v_p, hr_p, hi_p, _ = trunk_layer(xp, pos_p, h0_zero, h0_zero, None, None, lp)
        past_k = cache_k[l][page_table].reshape(bs, past_len, N_HEADS, HEAD_DIM)
        past_v = cache_v[l][page_table].reshape(bs, past_len, N_HEADS, HEAD_DIM)
        xs, k_s, v_s, hr_s, hi_s, sv_s = trunk_layer(xs, pos_s, state_ssm_re[l], state_ssm_im[l],
                                                     past_k, past_v, lp)
        kp_l.append(k_p); vp_l.append(v_p); hrp_l.append(hr_p); hip_l.append(hi_p)
        ks_l.append(k_s); vs_l.append(v_s); hrs_l.append(hr_s); his_l.append(hi_s); sv_l.append(sv_s)
    return (xp, xs,
            jnp.stack(kp_l), jnp.stack(vp_l), jnp.stack(hrp_l), jnp.stack(hip_l),
            jnp.stack(ks_l), jnp.stack(vs_l), jnp.stack(hrs_l), jnp.stack(his_l), jnp.stack(sv_l))
```

```python
import functools
import math

import jax
import jax.numpy as jnp
import numpy as np
from jax import lax
from jax.experimental import pallas as pl
from jax.experimental.pallas import tpu as pltpu

F32 = jnp.float32
BF16 = jnp.bfloat16

SSM_GROUP = 16
SSM_STATE = 64
CHUNK = 128
SGU_GROUPS = 4
HEAD_DIM = 64
MOBA_BLOCK = 256
MOBA_TOPK = 3
PAGE_SIZE = 128
ROPE_THETA = 10000.0
DT_EPS_LAM = -1e-4
EPS = 1e-6
NEG = -1e30

LANES = 128
SUBLANES = 8
VMEM_LIMIT = 56 * 1024 * 1024

PAGES_PER_STEP = 16


def _cparams(sem):
    return pltpu.CompilerParams(dimension_semantics=sem, vmem_limit_bytes=VMEM_LIMIT)


def _resident(block_shape, index_map):
    return pl.BlockSpec(block_shape, index_map, pipeline_mode=pl.Buffered(1))


def _rmsnorm(x, g):
    return x * lax.rsqrt(jnp.mean(x * x, axis=-1, keepdims=True) + EPS) * g


def _layernorm(v, g, b):
    mu = jnp.mean(v, axis=-1, keepdims=True)
    d = v - mu
    var = jnp.mean(d * d, axis=-1, keepdims=True)
    return d * lax.rsqrt(var + EPS) * g + b


def _gelu_tanh(x):
    c = math.sqrt(2.0 / math.pi)
    return 0.5 * x * (1.0 + jnp.tanh(c * (x + 0.044715 * (x * x * x))))


def _rope(x, cos, sin_signed):
    n = x.shape[-1]
    half = HEAD_DIM // 2
    lane = lax.broadcasted_iota(jnp.int32, x.shape, 1)
    first = (lane & (HEAD_DIM - 1)) < half
    swapped = jnp.where(first, pltpu.roll(x, n - half, axis=1), pltpu.roll(x, half, axis=1))
    return x * cos + swapped * sin_signed


def _dot(a, b):
    return jnp.dot(a, b, preferred_element_type=F32)


def _dot_nt(a, b, precision=None):
    return lax.dot_general(a, b, (((1,), (1,)), ((), ())), preferred_element_type=F32,
                           precision=precision)


def _ssm_expand(u, rep_ref, bre_ref, bim_ref):
    um = _dot(rep_ref[...], u.astype(BF16))
    row = lax.broadcasted_iota(jnp.int32, um.shape, 0)
    col = lax.broadcasted_iota(jnp.int32, um.shape, 1)
    keep = (row & (SUBLANES - 1)) == (col >> 5)
    um = jnp.where(keep, um, 0.0).astype(BF16)
    return _dot(um, bre_ref[...]), _dot(um, bim_ref[...])


def _ssm_readout(h_re, h_im, cre_ref, cim_ref):
    y2 = _dot(h_re.astype(BF16), cre_ref[...]) - _dot(h_im.astype(BF16), cim_ref[...])
    row = lax.broadcasted_iota(jnp.int32, y2.shape, 0)
    col = lax.broadcasted_iota(jnp.int32, y2.shape, 1)
    keep = (row & (SUBLANES - 1)) == (col >> 5)
    y2 = jnp.where(keep, y2, 0.0)
    n = y2.shape[0] // SUBLANES
    return jnp.sum(y2.reshape(n, SUBLANES, y2.shape[1]), axis=1)


def _ssm_glu(y, u, d_ref, wglu_ref, bglu_ref):
    y = y + d_ref[...] * u
    g = _gelu_tanh(y)
    return g * jax.nn.sigmoid(_dot(g.astype(BF16), wglu_ref[...]) + bglu_ref[...])


def _inproj_kernel(x_ref, gpre_ref, w_ref, cos_ref, sin_ref, lng_ref, lnb_ref, ws_ref, bs_ref,
                   u_ref, yb_ref, q_ref, k_ref, v_ref, kbf_ref, vbf_ref, gs_ref, km_ref):
    tm = x_ref.shape[0]
    h = _rmsnorm(x_ref[...], gpre_ref[...]).astype(BF16)

    def seg(a, b):
        return _dot(h, w_ref[:, a:b])

    u_ref[...] = seg(0, 256)

    u_sgu = seg(256, 512)
    vn = _layernorm(seg(512, 768), lng_ref[...], lnb_ref[...])
    r = lax.broadcasted_iota(jnp.int32, (CHUNK, CHUNK), 0)
    c = lax.broadcasted_iota(jnp.int32, (CHUNK, CHUNK), 1)
    tril = c <= r
    grp = lax.broadcasted_iota(jnp.int32, (CHUNK, 256), 1) >> 6
    wmask = [jnp.where(tril, ws_ref[g], 0.0).astype(BF16) for g in range(SGU_GROUPS)]
    for ci in range(tm // CHUNK):
        rows = slice(ci * CHUNK, (ci + 1) * CHUNK)
        vc = vn[rows]
        mixed = bs_ref[...]
        for g in range(SGU_GROUPS):
            mixed = mixed + _dot(wmask[g], jnp.where(grp == g, vc, 0.0).astype(BF16))
        yb_ref[rows, :] = (u_sgu[rows] * mixed).astype(BF16)

    cos = cos_ref[...]
    sin = sin_ref[...]
    q_ref[...] = _rope(seg(768, 1280), cos, sin)
    k = _rope(seg(1280, 1792), cos, sin)
    k_ref[...] = k
    kbf_ref[...] = k.astype(BF16)
    km_ref[0] = jnp.mean(k, axis=0, keepdims=True)
    v = seg(1792, 2304)
    v_ref[...] = v
    vbf_ref[...] = v.astype(BF16)
    for j in range(3):
        gs_ref[:, j * 1024:(j + 1) * 1024] = jax.nn.sigmoid(
            seg(2304 + j * 1024, 2304 + (j + 1) * 1024)).astype(BF16)


def _inproj(x2d, l, w_in_bf, gpre, cos_t, sin_t, lng, lnb, sgu_w, bs2d, seq_len):
    m, d = x2d.shape
    tm = MOBA_BLOCK
    n_t = seq_len // tm
    in_w = w_in_bf.shape[-1]
    row = lambda i: (i, 0)
    const2 = lambda i: (0, 0)
    outs = (
        jax.ShapeDtypeStruct((m, 256), F32),
        jax.ShapeDtypeStruct((m, 256), BF16),
        jax.ShapeDtypeStruct((m, 512), F32),
        jax.ShapeDtypeStruct((m, 512), F32),
        jax.ShapeDtypeStruct((m, 512), F32),
        jax.ShapeDtypeStruct((m, 512), BF16),
        jax.ShapeDtypeStruct((m, 512), BF16),
        jax.ShapeDtypeStruct((m, 3072), BF16),
        jax.ShapeDtypeStruct((m // tm, 1, 512), F32),
    )
    return pl.pallas_call(
        _inproj_kernel,
        grid=(m // tm,),
        in_specs=[
            pl.BlockSpec((tm, d), row),
            pl.BlockSpec((None, 1, d), lambda i: (l, 0, 0)),
            _resident((None, d, in_w), lambda i: (l, 0, 0)),
            pl.BlockSpec((tm, 512), lambda i: (i % n_t, 0)),
            pl.BlockSpec((tm, 512), lambda i: (i % n_t, 0)),
            pl.BlockSpec((None, 1, 256), lambda i: (l, 0, 0)),
            pl.BlockSpec((None, 1, 256), lambda i: (l, 0, 0)),
            pl.BlockSpec((None, SGU_GROUPS, CHUNK, CHUNK), lambda i: (l, 0, 0, 0)),
            pl.BlockSpec((None, CHUNK, 256), lambda i: (l, 0, 0)),
        ],
        out_specs=[
            pl.BlockSpec((tm, 256), row), pl.BlockSpec((tm, 256), row),
            pl.BlockSpec((tm, 512), row), pl.BlockSpec((tm, 512), row), pl.BlockSpec((tm, 512), row),
            pl.BlockSpec((tm, 512), row), pl.BlockSpec((tm, 512), row),
            pl.BlockSpec((tm, 3072), row),
            pl.BlockSpec((1, 1, 512), lambda i: (i, 0, 0)),
        ],
        out_shape=outs,
        compiler_params=_cparams(("arbitrary",)),
        name="inproj",
    )(x2d, gpre, w_in_bf, cos_t, sin_t, lng, lnb, sgu_w, bs2d)


def _ssm_kernel(u_ref, rep_ref, bre_ref, bim_ref, cre_ref, cim_ref, ar_ref, ai_ref,
                d_ref, wglu_ref, bglu_ref, ya_ref, hre_ref, him_ref, sre, sim):
    nb, tt = u_ref.shape[0], u_ref.shape[1]
    step = pl.program_id(0)

    @pl.when(step == 0)
    def _():
        hre_ref[...] = jnp.zeros_like(hre_ref)
        him_ref[...] = jnp.zeros_like(him_ref)

    for b in range(nb):
        bu_re, bu_im = _ssm_expand(u_ref[b], rep_ref, bre_ref, bim_ref)
        sre[b] = bu_re
        sim[b] = bu_im

    ar = ar_ref[...]
    ai = ai_ref[...]

    def body(t, carry):
        off = pl.multiple_of(t * SUBLANES, SUBLANES)
        new = []
        for b in range(nb):
            hr, hi = carry[2 * b], carry[2 * b + 1]
            nr = ar * hr - ai * hi + sre[b, pl.ds(off, SUBLANES), :]
            ni = ar * hi + ai * hr + sim[b, pl.ds(off, SUBLANES), :]
            sre[b, pl.ds(off, SUBLANES), :] = nr
            sim[b, pl.ds(off, SUBLANES), :] = ni
            new += [nr, ni]
        return tuple(new)

    init = []
    for b in range(nb):
        init += [hre_ref[b], him_ref[b]]
    fin = lax.fori_loop(0, tt, body, tuple(init), unroll=8)
    for b in range(nb):
        hre_ref[b] = fin[2 * b]
        him_ref[b] = fin[2 * b + 1]

    for b in range(nb):
        y = _ssm_readout(sre[b], sim[b], cre_ref, cim_ref)
        ya_ref[b] = _ssm_glu(y, u_ref[b], d_ref, wglu_ref, bglu_ref).astype(BF16)


def _ssm_scan(u3, l, rep, bre, bim, cre, cim, ar, ai, d_skip, wglu_bf, bglu):
    nb, t, w = u3.shape
    tt = rep.shape[1]
    c2 = lambda i: (0, 0)
    lyr3 = lambda i: (l, 0, 0)
    return pl.pallas_call(
        _ssm_kernel,
        grid=(t // tt,),
        in_specs=[
            pl.BlockSpec((nb, tt, w), lambda i: (0, i, 0)),
            pl.BlockSpec(rep.shape, c2),
            pl.BlockSpec((None, 256, LANES), lyr3), pl.BlockSpec((None, 256, LANES), lyr3),
            pl.BlockSpec((None, LANES, 256), lyr3), pl.BlockSpec((None, LANES, 256), lyr3),
            pl.BlockSpec((None, SUBLANES, LANES), lyr3), pl.BlockSpec((None, SUBLANES, LANES), lyr3),
            pl.BlockSpec((None, 1, 256), lyr3),
            pl.BlockSpec((None, 256, 256), lyr3),
            pl.BlockSpec((None, 1, 256), lyr3),
        ],
        out_specs=[
            pl.BlockSpec((nb, tt, w), lambda i: (0, i, 0)),
            pl.BlockSpec((nb, SUBLANES, LANES), lambda i: (0, 0, 0)),
            pl.BlockSpec((nb, SUBLANES, LANES), lambda i: (0, 0, 0)),
        ],
        out_shape=(
            jax.ShapeDtypeStruct((nb, t, w), BF16),
            jax.ShapeDtypeStruct((nb, SUBLANES, LANES), F32),
            jax.ShapeDtypeStruct((nb, SUBLANES, LANES), F32),
        ),
        scratch_shapes=[pltpu.VMEM((nb, SUBLANES * tt, LANES), F32),
                        pltpu.VMEM((nb, SUBLANES * tt, LANES), F32)],
        compiler_params=_cparams(("arbitrary",)),
        name="ssm_scan",
    )(u3, rep, bre, bim, cre, cim, ar, ai, d_skip, wglu_bf, bglu)


def _top3_mask(gate, idx, own):
    big = jnp.int32(1 << 20)
    g = jnp.where(idx < own, gate, NEG)
    sel = jnp.zeros(gate.shape, jnp.bool_)
    for _ in range(MOBA_TOPK):
        m = jnp.max(g, axis=-1, keepdims=True)
        first = jnp.min(jnp.where(g == m, idx, big), axis=-1, keepdims=True)
        pick = idx == first
        sel = sel | (pick & (idx < own))
        g = jnp.where(pick, -jnp.inf, g)
    return sel


def _select_kernel(q_ref, km_ref, sel_ref):
    own = pl.program_id(1)
    q = q_ref[0]
    km = km_ref[0]
    n_heads = q.shape[1] // HEAD_DIM
    lane_head = lax.broadcasted_iota(jnp.int32, q.shape, 1) >> 6
    for h in range(n_heads):
        qh = jnp.where(lane_head == h, q, 0.0)
        gate = _dot_nt(qh, km, precision=lax.Precision.HIGHEST)
        idx = lax.broadcasted_iota(jnp.int32, gate.shape, 1)
        sel_ref[0, h] = _top3_mask(gate, idx, own).astype(F32)


def _select(q3, km3):
    nb, t, w = q3.shape
    nblk = km3.shape[1]
    n_heads = w // HEAD_DIM
    tq = MOBA_BLOCK
    return pl.pallas_call(
        _select_kernel,
        grid=(nb, t // tq),
        in_specs=[pl.BlockSpec((1, tq, w), lambda b, i: (b, i, 0)),
                  pl.BlockSpec((1, nblk, w), lambda b, i: (b, 0, 0))],
        out_specs=pl.BlockSpec((1, n_heads, tq, nblk), lambda b, i: (b, 0, i, 0)),
        out_shape=jax.ShapeDtypeStruct((nb, n_heads, t, nblk), F32),
        compiler_params=_cparams(("arbitrary", "arbitrary")),
        name="moba_select",
    )(q3, km3)


def _attn_kernel(q_ref, k_ref, v_ref, sel_ref, o_ref):
    qi = pl.program_id(2)
    tq = q_ref.shape[1]
    scale = HEAD_DIM ** -0.5
    q = (q_ref[0] * scale).astype(BF16)
    lane = lax.broadcasted_iota(jnp.int32, q.shape, 1)
    outs = []
    for e in range(2):
        mine = (lane >> 6) == e
        qe = jnp.where(mine, q, jnp.zeros_like(q))
        sel = sel_ref[0, e]
        blk = lax.broadcasted_iota(jnp.int32, sel.shape, 1)

        d0 = pl.multiple_of(qi * MOBA_BLOCK, MOBA_BLOCK)
        kd = k_ref[0, pl.ds(d0, MOBA_BLOCK), :]
        vd = v_ref[0, pl.ds(d0, MOBA_BLOCK), :]
        s = _dot_nt(qe, kd)
        r = lax.broadcasted_iota(jnp.int32, s.shape, 0)
        c = lax.broadcasted_iota(jnp.int32, s.shape, 1)
        s = jnp.where(c <= r, s, NEG)
        m0 = jnp.max(s, axis=-1, keepdims=True)
        p = jnp.exp(s - m0)
        l0 = jnp.sum(p, axis=-1, keepdims=True)
        acc0 = _dot(p.astype(BF16), vd)

        def body(n, carry):
            m, l, acc = carry
            off = pl.multiple_of(n * MOBA_BLOCK, MOBA_BLOCK)
            kn = k_ref[0, pl.ds(off, MOBA_BLOCK), :]
            vn = v_ref[0, pl.ds(off, MOBA_BLOCK), :]
            allowed = jnp.sum(jnp.where(blk == n, sel, 0.0), axis=-1, keepdims=True) > 0.5
            sn = jnp.where(allowed, _dot_nt(qe, kn), NEG)
            m_new = jnp.maximum(m, jnp.max(sn, axis=-1, keepdims=True))
            alpha = jnp.exp(m - m_new)
            pn = jnp.exp(sn - m_new)
            l_new = alpha * l + jnp.sum(pn, axis=-1, keepdims=True)
            acc_new = alpha * acc + _dot(pn.astype(BF16), vn)
            return m_new, l_new, acc_new

        m, l, acc = lax.fori_loop(0, qi, body, (m0, l0, acc0))
        outs.append(acc / l)
    o_ref[0] = jnp.where((lane >> 6) == 0, outs[0], outs[1]).astype(BF16)


def _attention(q3, kbf3, vbf3, sel4):
    nb, t, w = q3.shape
    nblk = sel4.shape[-1]
    tq = MOBA_BLOCK
    return pl.pallas_call(
        _attn_kernel,
        grid=(nb, w // LANES, t // tq),
        in_specs=[
            pl.BlockSpec((1, tq, LANES), lambda b, hp, i: (b, i, hp)),
            pl.BlockSpec((1, t, LANES), lambda b, hp, i: (b, 0, hp)),
            pl.BlockSpec((1, t, LANES), lambda b, hp, i: (b, 0, hp)),
            pl.BlockSpec((1, 2, tq, nblk), lambda b, hp, i: (b, hp, i, 0)),
        ],
        out_specs=pl.BlockSpec((1, tq, LANES), lambda b, hp, i: (b, i, hp)),
        out_shape=jax.ShapeDtypeStruct((nb, t, w), BF16),
        compiler_params=_cparams(("arbitrary", "arbitrary", "arbitrary")),
        name="moba_attention",
    )(q3, kbf3, vbf3, sel4)


def _merge_kernel(x_ref, ya_ref, yb_ref, yc_ref, gs_ref, wa_ref, wb_ref, wc_ref, wo_ref, gpost_ref, o_ref):
    d = x_ref.shape[1]
    merged = (gs_ref[:, 0:d].astype(F32) * _dot(ya_ref[...], wa_ref[...])
              + gs_ref[:, d:2 * d].astype(F32) * _dot(yb_ref[...], wb_ref[...])
              + gs_ref[:, 2 * d:3 * d].astype(F32) * _dot(yc_ref[...], wc_ref[...]))
    mix = _dot(merged.astype(BF16), wo_ref[...])
    o_ref[...] = x_ref[...] + _rmsnorm(mix, gpost_ref[...])


def _merge(x2d, ya, yb, yc, gs, l, wa, wb, wc, wo, gpost, tm):
    m, d = x2d.shape
    row = lambda i: (i, 0)
    lyr = lambda i: (l, 0, 0)
    return pl.pallas_call(
        _merge_kernel,
        grid=(m // tm,),
        in_specs=[
            pl.BlockSpec((tm, d), row),
            pl.BlockSpec((tm, 256), row), pl.BlockSpec((tm, 256), row), pl.BlockSpec((tm, 512), row),
            pl.BlockSpec((tm, 3 * d), row),
            _resident((None, 256, d), lyr), _resident((None, 256, d), lyr),
            _resident((None, 512, d), lyr), _resident((None, d, d), lyr),
            pl.BlockSpec((None, 1, d), lyr),
        ],
        out_specs=pl.BlockSpec((tm, d), row),
        out_shape=jax.ShapeDtypeStruct((m, d), F32),
        compiler_params=_cparams(("arbitrary",)),
        name="merge",
    )(x2d, ya, yb, yc, gs, wa, wb, wc, wo, gpost)


def _ffn_kernel(x_ref, gpre_ref, wup_ref, wdn_ref, gpost_ref, o_ref):
    x = x_ref[...]
    h = _rmsnorm(x, gpre_ref[...]).astype(BF16)
    d = x.shape[1]
    dff = wup_ref.shape[1]
    acc = jnp.zeros(x.shape, F32)
    for j in range(dff // d):
        a = jnp.maximum(_dot(h, wup_ref[:, j * d:(j + 1) * d]), 0.0)
        acc = acc + _dot((a * a).astype(BF16), wdn_ref[j * d:(j + 1) * d, :])
    o_ref[...] = x + _rmsnorm(acc, gpost_ref[...])


def _ffn(x2d, l, gpre, wup, wdn, gpost, tm):
    m, d = x2d.shape
    dff = wup.shape[-1]
    row = lambda i: (i, 0)
    lyr = lambda i: (l, 0, 0)
    return pl.pallas_call(
        _ffn_kernel,
        grid=(m // tm,),
        in_specs=[
            pl.BlockSpec((tm, d), row),
            pl.BlockSpec((None, 1, d), lyr),
            _resident((None, d, dff), lyr), _resident((None, dff, d), lyr),
            pl.BlockSpec((None, 1, d), lyr),
        ],
        out_specs=pl.BlockSpec((tm, d), row),
        out_shape=jax.ShapeDtypeStruct((m, d), F32),
        compiler_params=_cparams(("arbitrary",)),
        name="ffn",
    )(x2d, gpre, wup, wdn, gpost)


def _sample_pre_kernel(x_ref, gpre_ref, w_ref, cos_ref, sin_ref, lng_ref, lnb_ref, w00_ref, b0_ref,
                       rep_ref, bre_ref, bim_ref, cre_ref, cim_ref, ar_ref, ai_ref, h0r_ref, h0i_ref,
                       d_ref, wglu_ref, bglu_ref,
                       ya_ref, yb_ref, q_ref, k_ref, v_ref, gs_ref, hr_ref, hi_ref, vrow_ref):
    h = _rmsnorm(x_ref[...], gpre_ref[...]).astype(BF16)

    def seg(a, b):
        return _dot(h, w_ref[:, a:b])

    u = seg(0, 256)
    bu_re, bu_im = _ssm_expand(u, rep_ref, bre_ref, bim_ref)
    ar, ai = ar_ref[...], ai_ref[...]
    h0r, h0i = h0r_ref[...], h0i_ref[...]
    hr = ar * h0r - ai * h0i + bu_re
    hi = ar * h0i + ai * h0r + bu_im
    hr_ref[...] = hr
    hi_ref[...] = hi
    y = _ssm_readout(hr, hi, cre_ref, cim_ref)
    ya_ref[...] = _ssm_glu(y, u, d_ref, wglu_ref, bglu_ref).astype(BF16)

    vn = _layernorm(seg(512, 768), lng_ref[...], lnb_ref[...])
    vrow_ref[...] = vn
    yb_ref[...] = (seg(256, 512) * (w00_ref[...] * vn + b0_ref[...])).astype(BF16)

    cos, sin = cos_ref[...], sin_ref[...]
    q_ref[...] = _rope(seg(768, 1280), cos, sin)
    k_ref[...] = _rope(seg(1280, 1792), cos, sin)
    v_ref[...] = seg(1792, 2304)
    for j in range(3):
        gs_ref[:, j * 1024:(j + 1) * 1024] = jax.nn.sigmoid(
            seg(2304 + j * 1024, 2304 + (j + 1) * 1024)).astype(BF16)


def _sample_pre(xs, l, w_in_bf, gpre, cos_s, sin_s, lng, lnb, w00, b0, rep_s, bre, bim, cre, cim,
                ar_s, ai_s, h0r, h0i, d_skip, wglu_bf, bglu):
    ns, d = xs.shape
    in_w = w_in_bf.shape[-1]
    c2 = lambda i: (0, 0)
    lyr = lambda i: (l, 0, 0)
    full = lambda a: pl.BlockSpec(a.shape, c2)
    outs = (
        jax.ShapeDtypeStruct((ns, 256), BF16), jax.ShapeDtypeStruct((ns, 256), BF16),
        jax.ShapeDtypeStruct((ns, 512), F32), jax.ShapeDtypeStruct((ns, 512), F32),
        jax.ShapeDtypeStruct((ns, 512), F32), jax.ShapeDtypeStruct((ns, 3072), BF16),
        jax.ShapeDtypeStruct((ns * SUBLANES, LANES), F32), jax.ShapeDtypeStruct((ns * SUBLANES, LANES), F32),
        jax.ShapeDtypeStruct((ns, 256), F32),
    )
    return pl.pallas_call(
        _sample_pre_kernel,
        grid=(1,),
        in_specs=[
            full(xs),
            pl.BlockSpec((None, 1, d), lyr),
            pl.BlockSpec((None, d, in_w), lyr),
            full(cos_s), full(sin_s),
            pl.BlockSpec((None, 1, 256), lyr), pl.BlockSpec((None, 1, 256), lyr),
            pl.BlockSpec((None, 1, 256), lyr), pl.BlockSpec((None, 1, 256), lyr),
            full(rep_s),
            pl.BlockSpec((None, 256, LANES), lyr), pl.BlockSpec((None, 256, LANES), lyr),
            pl.BlockSpec((None, LANES, 256), lyr), pl.BlockSpec((None, LANES, 256), lyr),
            pl.BlockSpec((None, ns * SUBLANES, LANES), lyr), pl.BlockSpec((None, ns * SUBLANES, LANES), lyr),
            pl.BlockSpec((None, ns * SUBLANES, LANES), lyr), pl.BlockSpec((None, ns * SUBLANES, LANES), lyr),
            pl.BlockSpec((None, 1, 256), lyr),
            pl.BlockSpec((None, 256, 256), lyr),
            pl.BlockSpec((None, 1, 256), lyr),
        ],
        out_specs=[pl.BlockSpec(o.shape, c2) for o in outs],
        out_shape=outs,
        compiler_params=_cparams(("arbitrary",)),
        name="sample_pre",
    )(xs, gpre, w_in_bf, cos_s, sin_s, lng, lnb, w00, b0, rep_s, bre, bim, cre, cim,
      ar_s, ai_s, h0r, h0i, d_skip, wglu_bf, bglu)


def _cache_mean_kernel(pt_ref, *refs):
    del pt_ref
    pages, o_ref = refs[:-1], refs[-1]
    per_blk = MOBA_BLOCK // PAGE_SIZE
    for i in range(len(pages) // per_blk):
        s = jnp.sum(pages[per_blk * i][...], axis=0, keepdims=True)
        for j in range(1, per_blk):
            s = s + jnp.sum(pages[per_blk * i + j][...], axis=0, keepdims=True)
        o_ref[0, i:i + 1, :] = s * (1.0 / MOBA_BLOCK)


def _cache_means(cache_k4, page_table, l):
    ns, n_pages = page_table.shape
    w = cache_k4.shape[-1]
    pp = PAGES_PER_STEP
    per_blk = MOBA_BLOCK // PAGE_SIZE
    nblk = n_pages // per_blk

    def page_spec(i):
        return pl.BlockSpec((None, None, PAGE_SIZE, w),
                            lambda b, j, pt: (l, pt[b, j * pp + i], 0, 0))

    grid_spec = pltpu.PrefetchScalarGridSpec(
        num_scalar_prefetch=1,
        grid=(ns, n_pages // pp),
        in_specs=[page_spec(i) for i in range(pp)],
        out_specs=pl.BlockSpec((1, pp // per_blk, w), lambda b, j, pt: (b, j, 0)),
    )
    return pl.pallas_call(
        _cache_mean_kernel,
        grid_spec=grid_spec,
        out_shape=jax.ShapeDtypeStruct((ns, nblk, w), F32),
        compiler_params=_cparams(("arbitrary", "arbitrary")),
        name="cache_block_means",
    )(page_table, *([cache_k4] * pp))


def _sample_select_kernel(q_ref, km_ref, hind_ref, o_ref):
    big = jnp.int32(1 << 20)
    prod = km_ref[0] * q_ref[0]
    g = jnp.dot(prod, hind_ref[...], preferred_element_type=F32,
                precision=lax.Precision.HIGHEST)
    idx = lax.broadcasted_iota(jnp.int32, g.shape, 0)
    o_ref[0] = jnp.zeros(o_ref.shape[1:], jnp.int32)
    for r in range(MOBA_TOPK):
        m = jnp.max(g, axis=0, keepdims=True)
        first = jnp.min(jnp.where(g == m, idx, big), axis=0, keepdims=True)
        o_ref[0, r:r + 1, :] = first
        g = jnp.where(idx == first, -jnp.inf, g)


def _sample_select(q_s, km_s, hind):
    ns, w = q_s.shape
    nblk = km_s.shape[1]
    return pl.pallas_call(
        _sample_select_kernel,
        grid=(ns,),
        in_specs=[pl.BlockSpec((1, 1, w), lambda b: (b, 0, 0)),
                  pl.BlockSpec((1, nblk, w), lambda b: (b, 0, 0)),
                  pl.BlockSpec(hind.shape, lambda b: (0, 0))],
        out_specs=pl.BlockSpec((1, SUBLANES, LANES), lambda b: (b, 0, 0)),
        out_shape=jax.ShapeDtypeStruct((ns, SUBLANES, LANES), jnp.int32),
        compiler_params=_cparams(("arbitrary",)),
        name="sample_select",
    )(q_s.reshape(ns, 1, w), km_s, hind)


def _sample_attn_kernel(pt_ref, sel_ref, q_ref, kn_ref, vn_ref, ck_ref, cv_ref, o_ref,
                        kbuf, vbuf, sem, *, layer, n_heads):
    b = pl.program_id(0)
    per_blk = MOBA_BLOCK // PAGE_SIZE

    def copies(h, r, j):
        blk = sel_ref[(b * MOBA_TOPK + r) * n_heads + h]
        page = pt_ref[b, blk * per_blk + j]
        rows = pl.ds((r * per_blk + j) * PAGE_SIZE, PAGE_SIZE)
        return (pltpu.make_async_copy(ck_ref.at[layer, page, :, h, :], kbuf.at[h, rows, :], sem.at[0]),
                pltpu.make_async_copy(cv_ref.at[layer, page, :, h, :], vbuf.at[h, rows, :], sem.at[1]))

    for h in range(n_heads):
        for r in range(MOBA_TOPK):
            for j in range(per_blk):
                ck, cv = copies(h, r, j)
                ck.start()
                cv.start()
    for h in range(n_heads):
        for r in range(MOBA_TOPK):
            for j in range(per_blk):
                ck, cv = copies(h, r, j)
                ck.wait()
                cv.wait()

    scale = HEAD_DIM ** -0.5
    for h in range(n_heads):
        lanes = slice(h * HEAD_DIM, (h + 1) * HEAD_DIM)
        qh = q_ref[0, :, lanes] * scale
        s = jnp.sum(kbuf[h] * qh, axis=-1, keepdims=True)
        s_new = jnp.sum(kn_ref[0, :, lanes] * qh, axis=-1, keepdims=True)
        m = jnp.maximum(jnp.max(s, axis=0, keepdims=True), s_new)
        p = jnp.exp(s - m)
        p_new = jnp.exp(s_new - m)
        den = jnp.sum(p, axis=0, keepdims=True) + p_new
        num = jnp.sum(p * vbuf[h], axis=0, keepdims=True) + p_new * vn_ref[0, :, lanes]
        o_ref[0, :, lanes] = (num / den).astype(o_ref.dtype)


def _sample_attention(q_s, k_s, v_s, cache_k, cache_v, page_table, sel_flat, l):
    ns, w = q_s.shape
    n_heads = w // HEAD_DIM
    rows = MOBA_TOPK * MOBA_BLOCK
    q3, k3, v3 = (a.reshape(ns, 1, w) for a in (q_s, k_s, v_s))
    row = lambda b, pt, sel: (b, 0, 0)
    grid_spec = pltpu.PrefetchScalarGridSpec(
        num_scalar_prefetch=2,
        grid=(ns,),
        in_specs=[pl.BlockSpec((1, 1, w), row), pl.BlockSpec((1, 1, w), row), pl.BlockSpec((1, 1, w), row),
                  pl.BlockSpec(memory_space=pl.ANY), pl.BlockSpec(memory_space=pl.ANY)],
        out_specs=pl.BlockSpec((1, 1, w), row),
        scratch_shapes=[pltpu.VMEM((n_heads, rows, HEAD_DIM), F32),
                        pltpu.VMEM((n_heads, rows, HEAD_DIM), F32),
                        pltpu.SemaphoreType.DMA((2,))],
    )
    out = pl.pallas_call(
        functools.partial(_sample_attn_kernel, layer=l, n_heads=n_heads),
        grid_spec=grid_spec,
        out_shape=jax.ShapeDtypeStruct((ns, 1, w), BF16),
        compiler_params=_cparams(("arbitrary",)),
        name="sample_attention",
    )(page_table, sel_flat, q3, k3, v3, cache_k, cache_v)
    return out.reshape(ns, w)


def _rope_tables(pos, n_heads):
    half = HEAD_DIM // 2
    inv_freq = ROPE_THETA ** (-jnp.arange(half, dtype=F32) / half)
    ang = pos.astype(F32)[:, None] * inv_freq[None, :]
    cos, sin = jnp.cos(ang), jnp.sin(ang)
    cos_t = jnp.tile(jnp.concatenate([cos, cos], axis=-1), (1, n_heads))
    sin_t = jnp.tile(jnp.concatenate([-sin, sin], axis=-1), (1, n_heads))
    return cos_t, sin_t


def _ssm_tables(a_re, a_im, log_dt, b_re, b_im, c_re, c_im):
    depth, groups, p = a_re.shape
    lam_re = jnp.minimum(a_re.astype(F32), DT_EPS_LAM)
    lam_im = a_im.astype(F32)
    dt = jnp.exp(log_dt.astype(F32))[..., None]
    mag = jnp.exp(lam_re * dt)
    ab_re = mag * jnp.cos(lam_im * dt)
    ab_im = mag * jnp.sin(lam_im * dt)
    den = lam_re * lam_re + lam_im * lam_im
    num_re = ab_re - 1.0
    f_re = (num_re * lam_re + ab_im * lam_im) / den
    f_im = (ab_im * lam_re - num_re * lam_im) / den
    br, bi = b_re.astype(F32), b_im.astype(F32)
    bb_re = f_re[..., None] * br - f_im[..., None] * bi
    bb_im = f_re[..., None] * bi + f_im[..., None] * br
    eye2 = jnp.eye(2, dtype=F32)

    def b_stack(bb):
        t = bb.transpose(0, 1, 3, 2).reshape(depth, groups // 2, 2, SSM_GROUP, p)
        t = t[:, :, :, :, None, :] * eye2[None, None, :, None, :, None]
        return t.reshape(depth, groups * SSM_GROUP, 2 * p).astype(BF16)

    def c_stack(c):
        t = c.astype(F32).reshape(depth, groups // 2, 2, SSM_GROUP, p)
        t = t[:, :, :, :, None, :] * eye2[None, None, :, None, :, None]
        t = t.transpose(0, 4, 5, 1, 2, 3)
        return t.reshape(depth, 2 * p, groups * SSM_GROUP).astype(BF16)

    lay = lambda a: a.reshape(depth, SUBLANES, LANES)
    return (lay(ab_re), lay(ab_im), b_stack(bb_re), b_stack(bb_im), c_stack(c_re), c_stack(c_im))


def _repeat_matrix(n):
    return jnp.repeat(jnp.eye(n, dtype=BF16), SUBLANES, axis=0)


def kernel(x_prompt, x_sample, cache_k, cache_v, state_ssm_re, state_ssm_im, page_table, mix_pre_g, w_in, ssm_a_re, ssm_a_im, ssm_log_dt, ssm_b_re, ssm_b_im, ssm_c_re, ssm_c_im, ssm_d, ssm_w_glu, ssm_b_glu, sgu_ln_g, sgu_ln_b, sgu_w, sgu_b, w_out_a, w_out_b, w_out_c, w_o, mix_post_g, ffn_pre_g, w_up, w_down, ffn_post_g):
    bp, tp, d = x_prompt.shape
    bs, ts, _ = x_sample.shape
    depth = w_in.shape[0]
    n_pool = cache_k.shape[1]
    n_heads = cache_k.shape[3]
    n_pages = page_table.shape[1]
    past_len = n_pages * PAGE_SIZE
    groups = ssm_a_re.shape[1]
    assert ts == 1 and tp % MOBA_BLOCK == 0 and past_len % MOBA_BLOCK == 0
    assert n_pages % PAGES_PER_STEP == 0 and groups * SSM_STATE == SUBLANES * LANES

    row3 = lambda a: a.reshape(depth, 1, a.shape[-1]).astype(F32)
    w_in_bf = w_in.astype(BF16)
    wa_bf, wb_bf, wc_bf, wo_bf = (w.astype(BF16) for w in (w_out_a, w_out_b, w_out_c, w_o))
    wup_bf, wdn_bf = w_up.astype(BF16), w_down.astype(BF16)
    wglu_bf = ssm_w_glu.astype(BF16)
    gpre, gpost, fpre, fpost = row3(mix_pre_g), row3(mix_post_g), row3(ffn_pre_g), row3(ffn_post_g)
    lng, lnb, d_skip, bglu = row3(sgu_ln_g), row3(sgu_ln_b), row3(ssm_d), row3(ssm_b_glu)
    cos_p, sin_p = _rope_tables(jnp.arange(tp, dtype=jnp.int32), n_heads)
    cos_s, sin_s = _rope_tables(jnp.full((bs,), past_len, dtype=jnp.int32), n_heads)
    ar, ai, bre, bim, cre, cim = _ssm_tables(ssm_a_re, ssm_a_im, ssm_log_dt, ssm_b_re, ssm_b_im,
                                             ssm_c_re, ssm_c_im)
    sgu_dim = sgu_ln_g.shape[-1] // SGU_GROUPS
    bs2d = jnp.repeat(sgu_b.astype(F32).transpose(0, 2, 1), sgu_dim, axis=-1)
    w00 = jnp.repeat(sgu_w[:, :, 0, 0].astype(F32), sgu_dim, axis=-1).reshape(depth, 1, -1)
    b0 = bs2d[:, 0:1, :]
    tt = 256
    rep_p = _repeat_matrix(tt)
    rep_s = _repeat_matrix(bs)
    ar_s, ai_s = jnp.tile(ar, (1, bs, 1)), jnp.tile(ai, (1, bs, 1))
    h0r = state_ssm_re.astype(F32).reshape(depth, bs * SUBLANES, LANES)
    h0i = state_ssm_im.astype(F32).reshape(depth, bs * SUBLANES, LANES)
    cache_k4 = cache_k.reshape(depth, n_pool, PAGE_SIZE, n_heads * HEAD_DIM)
    hind = (jnp.arange(n_heads * HEAD_DIM)[:, None] // HEAD_DIM == jnp.arange(LANES)[None, :]).astype(F32)

    xp = x_prompt.reshape(bp * tp, d)
    xs = x_sample.reshape(bs * ts, d)
    kp_l, vp_l, hrp_l, hip_l = [], [], [], []
    ks_l, vs_l, hrs_l, his_l, sv_l = [], [], [], [], []
    for l in range(depth):
        u, yb, q, k, v, kbf, vbf, gs, km = _inproj(xp, l, w_in_bf, gpre, cos_p, sin_p, lng, lnb,
                                                   sgu_w, bs2d, tp)
        ya, hre, him = _ssm_scan(u.reshape(bp, tp, -1), l, rep_p, bre, bim, cre, cim, ar, ai,
                                 d_skip, wglu_bf, bglu)
        q3 = q.reshape(bp, tp, -1)
        sel = _select(q3, km.reshape(bp, tp // MOBA_BLOCK, -1))
        yc = _attention(q3, kbf.reshape(bp, tp, -1), vbf.reshape(bp, tp, -1), sel)
        xp = _merge(xp, ya.reshape(bp * tp, -1), yb, yc.reshape(bp * tp, -1), gs, l,
                    wa_bf, wb_bf, wc_bf, wo_bf, gpost, 512)
        xp = _ffn(xp, l, fpre, wup_bf, wdn_bf, fpost, 512)
        kp_l.append(k.reshape(bp, tp, n_heads, HEAD_DIM))
        vp_l.append(v.reshape(bp, tp, n_heads, HEAD_DIM))
        hrp_l.append(hre.reshape(bp, groups, SSM_STATE))
        hip_l.append(him.reshape(bp, groups, SSM_STATE))

        (ya_s, yb_s, q_s, k_s, v_s, gs_s, hr_s, hi_s, vrow) = _sample_pre(
            xs, l, w_in_bf, gpre, cos_s, sin_s, lng, lnb, w00, b0, rep_s, bre, bim, cre, cim,
            ar_s, ai_s, h0r, h0i, d_skip, wglu_bf, bglu)
        km_s = _cache_means(cache_k4, page_table, l)
        sel_s = _sample_select(q_s, km_s, hind)
        sel_flat = sel_s[:, :MOBA_TOPK, :n_heads].reshape(-1)
        yc_s = _sample_attention(q_s, k_s, v_s, cache_k, cache_v, page_table, sel_flat, l)
        xs = _merge(xs, ya_s, yb_s, yc_s, gs_s, l, wa_bf, wb_bf, wc_bf, wo_bf, gpost, bs)
        xs = _ffn(xs, l, fpre, wup_bf, wdn_bf, fpost, bs)
        ks_l.append(k_s.reshape(bs, ts, n_heads, HEAD_DIM))
        vs_l.append(v_s.reshape(bs, ts, n_heads, HEAD_DIM))
        hrs_l.append(hr_s.reshape(bs, groups, SSM_STATE))
        his_l.append(hi_s.reshape(bs, groups, SSM_STATE))
        sv_l.append(vrow.reshape(bs, ts, -1))

    return (xp.reshape(bp, tp, d), xs.reshape(bs, ts, d),
            jnp.stack(kp_l), jnp.stack(vp_l), jnp.stack(hrp_l), jnp.stack(hip_l),
            jnp.stack(ks_l), jnp.stack(vs_l), jnp.stack(hrs_l), jnp.stack(his_l), jnp.stack(sv_l))
```

```python
import functools
import math

import jax
import jax.numpy as jnp
from jax import lax
from jax.experimental import pallas as pl
from jax.experimental.pallas import tpu as pltpu

F32 = jnp.float32
BF16 = jnp.bfloat16

SSM_GROUP = 16
SSM_STATE = 64
CHUNK = 128
SGU_GROUPS = 4
HEAD_DIM = 64
MOBA_BLOCK = 256
MOBA_TOPK = 3
PAGE_SIZE = 128
ROPE_THETA = 10000.0
DT_EPS_LAM = -1e-4
EPS = 1e-6
NEG = -1e30

LANES = 128
SUBLANES = 8
VMEM_LIMIT = 56 * 1024 * 1024

PAGES_PER_STEP = 16
ATT_TILE = 512
SSM_TILE = 256


def _cparams(sem):
    return pltpu.CompilerParams(dimension_semantics=sem, vmem_limit_bytes=VMEM_LIMIT)


def _resident(block_shape, index_map):
    return pl.BlockSpec(block_shape, index_map, pipeline_mode=pl.Buffered(1))


def _rmsnorm(x, g):
    return x * lax.rsqrt(jnp.mean(x * x, axis=-1, keepdims=True) + EPS) * g


def _layernorm(v, g, b):
    mu = jnp.mean(v, axis=-1, keepdims=True)
    d = v - mu
    var = jnp.mean(d * d, axis=-1, keepdims=True)
    return d * lax.rsqrt(var + EPS) * g + b


def _gelu_tanh(x):
    c = math.sqrt(2.0 / math.pi)
    return 0.5 * x * (1.0 + jnp.tanh(c * (x + 0.044715 * (x * x * x))))


def _rope(x, cos, sin_signed):
    n = x.shape[-1]
    half = HEAD_DIM // 2
    lane = lax.broadcasted_iota(jnp.int32, x.shape, 1)
    first = (lane & (HEAD_DIM - 1)) < half
    swapped = jnp.where(first, pltpu.roll(x, n - half, axis=1), pltpu.roll(x, half, axis=1))
    return x * cos + swapped * sin_signed


def _dot(a, b):
    return jnp.dot(a, b, preferred_element_type=F32)


def _dot_nt(a, b, precision=None):
    return lax.dot_general(a, b, (((1,), (1,)), ((), ())), preferred_element_type=F32,
                           precision=precision)


def _ssm_expand(u, rep_ref, bre_ref, bim_ref):
    um = _dot(rep_ref[...], u.astype(BF16))
    row = lax.broadcasted_iota(jnp.int32, um.shape, 0)
    col = lax.broadcasted_iota(jnp.int32, um.shape, 1)
    keep = (row & (SUBLANES - 1)) == (col >> 5)
    um = jnp.where(keep, um, 0.0).astype(BF16)
    return _dot(um, bre_ref[...]), _dot(um, bim_ref[...])


def _ssm_readout(h_re, h_im, cre_ref, cim_ref):
    y2 = _dot(h_re.astype(BF16), cre_ref[...]) - _dot(h_im.astype(BF16), cim_ref[...])
    row = lax.broadcasted_iota(jnp.int32, y2.shape, 0)
    col = lax.broadcasted_iota(jnp.int32, y2.shape, 1)
    keep = (row & (SUBLANES - 1)) == (col >> 5)
    y2 = jnp.where(keep, y2, 0.0)
    n = y2.shape[0] // SUBLANES
    return jnp.sum(y2.reshape(n, SUBLANES, y2.shape[1]), axis=1)


def _ssm_glu(y, u, d_ref, wglu_ref, bglu_ref):
    y = y + d_ref[...] * u
    g = _gelu_tanh(y)
    return g * jax.nn.sigmoid(_dot(g.astype(BF16), wglu_ref[...]) + bglu_ref[...])


def _inproj_kernel(x_ref, gpre_ref, w_ref, wvt_ref, cos_ref, sin_ref, lng_ref, lnb_ref, ws_ref, bs_ref,
                   u_ref, yb_ref, q_ref, k_ref, v_ref, kbf_ref, vt_ref, gs_ref, km_ref):
    tm = x_ref.shape[0]
    h = _rmsnorm(x_ref[...], gpre_ref[...]).astype(BF16)

    def seg(a, b):
        return _dot(h, w_ref[:, a:b])

    u_ref[...] = seg(0, 256)

    u_sgu = seg(256, 512)
    vn = _layernorm(seg(512, 768), lng_ref[...], lnb_ref[...])
    r = lax.broadcasted_iota(jnp.int32, (CHUNK, CHUNK), 0)
    c = lax.broadcasted_iota(jnp.int32, (CHUNK, CHUNK), 1)
    tril = c <= r
    grp = lax.broadcasted_iota(jnp.int32, (CHUNK, 256), 1) >> 6
    wmask = [jnp.where(tril, ws_ref[g], 0.0).astype(BF16) for g in range(SGU_GROUPS)]
    for ci in range(tm // CHUNK):
        rows = slice(ci * CHUNK, (ci + 1) * CHUNK)
        vc = vn[rows]
        mixed = bs_ref[...]
        for g in range(SGU_GROUPS):
            mixed = mixed + _dot(wmask[g], jnp.where(grp == g, vc, 0.0).astype(BF16))
        yb_ref[rows, :] = (u_sgu[rows] * mixed).astype(BF16)

    cos = cos_ref[...]
    sin = sin_ref[...]
    q_ref[...] = _rope(seg(768, 1280), cos, sin)
    k = _rope(seg(1280, 1792), cos, sin)
    k_ref[...] = k
    kbf_ref[...] = k.astype(BF16)
    km_ref[0] = jnp.mean(k, axis=0, keepdims=True)
    v_ref[...] = seg(1792, 2304)
    vt_ref[...] = _dot_nt(wvt_ref[...], h).astype(BF16)
    for j in range(3):
        gs_ref[:, j * 1024:(j + 1) * 1024] = jax.nn.sigmoid(
            seg(2304 + j * 1024, 2304 + (j + 1) * 1024)).astype(BF16)


def _inproj(x2d, l, w_in_bf, wvt_bf, gpre, cos_t, sin_t, lng, lnb, sgu_w, bs2d, seq_len):
    m, d = x2d.shape
    tm = MOBA_BLOCK
    n_t = seq_len // tm
    in_w = w_in_bf.shape[-1]
    row = lambda i: (i, 0)
    outs = (
        jax.ShapeDtypeStruct((m, 256), F32),
        jax.ShapeDtypeStruct((m, 256), BF16),
        jax.ShapeDtypeStruct((m, 512), F32),
        jax.ShapeDtypeStruct((m, 512), F32),
        jax.ShapeDtypeStruct((m, 512), F32),
        jax.ShapeDtypeStruct((m, 512), BF16),
        jax.ShapeDtypeStruct((512, m), BF16),
        jax.ShapeDtypeStruct((m, 3072), BF16),
        jax.ShapeDtypeStruct((m // tm, 1, 512), F32),
    )
    return pl.pallas_call(
        _inproj_kernel,
        grid=(m // tm,),
        in_specs=[
            pl.BlockSpec((tm, d), row),
            pl.BlockSpec((None, 1, d), lambda i: (l, 0, 0)),
            _resident((None, d, in_w), lambda i: (l, 0, 0)),
            _resident((None, 512, d), lambda i: (l, 0, 0)),
            pl.BlockSpec((tm, 512), lambda i: (i % n_t, 0)),
            pl.BlockSpec((tm, 512), lambda i: (i % n_t, 0)),
            pl.BlockSpec((None, 1, 256), lambda i: (l, 0, 0)),
            pl.BlockSpec((None, 1, 256), lambda i: (l, 0, 0)),
            pl.BlockSpec((None, SGU_GROUPS, CHUNK, CHUNK), lambda i: (l, 0, 0, 0)),
            pl.BlockSpec((None, CHUNK, 256), lambda i: (l, 0, 0)),
        ],
        out_specs=[
            pl.BlockSpec((tm, 256), row), pl.BlockSpec((tm, 256), row),
            pl.BlockSpec((tm, 512), row), pl.BlockSpec((tm, 512), row), pl.BlockSpec((tm, 512), row),
            pl.BlockSpec((tm, 512), row), pl.BlockSpec((512, tm), lambda i: (0, i)),
            pl.BlockSpec((tm, 3072), row),
            pl.BlockSpec((1, 1, 512), lambda i: (i, 0, 0)),
        ],
        out_shape=outs,
        compiler_params=_cparams(("arbitrary",)),
        name="inproj",
    )(x2d, gpre, w_in_bf, wvt_bf, cos_t, sin_t, lng, lnb, sgu_w, bs2d)


def _ssm_kernel(u_ref, bre_ref, bim_ref, cre_ref, cim_ref, ar_ref, ai_ref,
                d_ref, wglu_ref, bglu_ref, ya_ref, hre_ref, him_ref, sre, sim):
    nb, tt = u_ref.shape[0], u_ref.shape[1]
    step = pl.program_id(0)

    @pl.when(step == 0)
    def _():
        hre_ref[...] = jnp.zeros_like(hre_ref)
        him_ref[...] = jnp.zeros_like(him_ref)

    for b in range(nb):
        ub = u_ref[b].astype(BF16)
        for j in range(SUBLANES):
            rows_j = pl.ds(j, tt, stride=SUBLANES)
            sre[b, rows_j, :] = _dot(ub, bre_ref[j])
            sim[b, rows_j, :] = _dot(ub, bim_ref[j])

    ar = ar_ref[...]
    ai = ai_ref[...]

    def body(t, carry):
        off = pl.multiple_of(t * SUBLANES, SUBLANES)
        new = []
        for b in range(nb):
            hr, hi = carry[2 * b], carry[2 * b + 1]
            nr = ar * hr - ai * hi + sre[b, pl.ds(off, SUBLANES), :]
            ni = ar * hi + ai * hr + sim[b, pl.ds(off, SUBLANES), :]
            sre[b, pl.ds(off, SUBLANES), :] = nr
            sim[b, pl.ds(off, SUBLANES), :] = ni
            new += [nr, ni]
        return tuple(new)

    init = []
    for b in range(nb):
        init += [hre_ref[b], him_ref[b]]
    fin = lax.fori_loop(0, tt, body, tuple(init), unroll=8)
    for b in range(nb):
        hre_ref[b] = fin[2 * b]
        him_ref[b] = fin[2 * b + 1]

    for b in range(nb):
        y = jnp.zeros((tt, u_ref.shape[2]), F32)
        for j in range(SUBLANES):
            rows_j = pl.ds(j, tt, stride=SUBLANES)
            y = y + _dot(sre[b, rows_j, :].astype(BF16), cre_ref[j])
            y = y - _dot(sim[b, rows_j, :].astype(BF16), cim_ref[j])
        ya_ref[b] = _ssm_glu(y, u_ref[b], d_ref, wglu_ref, bglu_ref).astype(BF16)


def _ssm_scan(u3, l, tt, bre8, bim8, cre8, cim8, ar, ai, d_skip, wglu_bf, bglu):
    nb, t, w = u3.shape
    lyr3 = lambda i: (l, 0, 0)
    lyr4 = lambda i: (l, 0, 0, 0)
    return pl.pallas_call(
        _ssm_kernel,
        grid=(t // tt,),
        in_specs=[
            pl.BlockSpec((nb, tt, w), lambda i: (0, i, 0)),
            pl.BlockSpec((None, SUBLANES, 256, LANES), lyr4), pl.BlockSpec((None, SUBLANES, 256, LANES), lyr4),
            pl.BlockSpec((None, SUBLANES, LANES, 256), lyr4), pl.BlockSpec((None, SUBLANES, LANES, 256), lyr4),
            pl.BlockSpec((None, SUBLANES, LANES), lyr3), pl.BlockSpec((None, SUBLANES, LANES), lyr3),
            pl.BlockSpec((None, 1, 256), lyr3),
            pl.BlockSpec((None, 256, 256), lyr3),
            pl.BlockSpec((None, 1, 256), lyr3),
        ],
        out_specs=[
            pl.BlockSpec((nb, tt, w), lambda i: (0, i, 0)),
            pl.BlockSpec((nb, SUBLANES, LANES), lambda i: (0, 0, 0)),
            pl.BlockSpec((nb, SUBLANES, LANES), lambda i: (0, 0, 0)),
        ],
        out_shape=(
            jax.ShapeDtypeStruct((nb, t, w), BF16),
            jax.ShapeDtypeStruct((nb, SUBLANES, LANES), F32),
            jax.ShapeDtypeStruct((nb, SUBLANES, LANES), F32),
        ),
        scratch_shapes=[pltpu.VMEM((nb, SUBLANES * tt, LANES), F32),
                        pltpu.VMEM((nb, SUBLANES * tt, LANES), F32)],
        compiler_params=_cparams(("arbitrary",)),
        name="ssm_scan",
    )(u3, bre8, bim8, cre8, cim8, ar, ai, d_skip, wglu_bf, bglu)


def _top3_rows(gate, blk, own):
    big = jnp.int32(1 << 20)
    g = jnp.where(blk < own, gate, NEG)
    sel = jnp.zeros(gate.shape, jnp.bool_)
    for _ in range(MOBA_TOPK):
        m = jnp.max(g, axis=0, keepdims=True)
        first = jnp.min(jnp.where(g == m, blk, big), axis=0, keepdims=True)
        pick = blk == first
        sel = sel | (pick & (blk < own))
        g = jnp.where(pick, -jnp.inf, g)
    return sel


def _split_bf16(x):
    hi = x.astype(BF16)
    return hi, (x - hi.astype(F32)).astype(BF16)


def _attn_kernel(q_ref, k_ref, vt_ref, km_ref, o_ref, kaug, qaug, acc):
    qt = pl.program_id(2)
    tq = q_ref.shape[1]
    tk = tq
    t = k_ref.shape[1]
    nblk = km_ref.shape[1]
    shift = MOBA_BLOCK.bit_length() - 1

    @pl.when(qt == 0)
    def _():
        def fill(i, carry):
            off = pl.multiple_of(i * tk, tk)
            kk = k_ref[0, pl.ds(off, tk), :].astype(F32)
            lane = lax.broadcasted_iota(jnp.int32, kk.shape, 1)
            blk = (lax.broadcasted_iota(jnp.int32, kk.shape, 0) + off) >> shift
            hot_a = jnp.where(lane - HEAD_DIM == blk, 1.0, 0.0)
            hot_b = jnp.where(lane == blk, 1.0, 0.0)
            kaug[0, pl.ds(off, tk), :] = jnp.where(lane < HEAD_DIM, kk, hot_a).astype(BF16)
            kaug[1, pl.ds(off, tk), :] = jnp.where(lane >= HEAD_DIM, kk, hot_b).astype(BF16)
            return carry
        lax.fori_loop(0, t // tk, fill, 0)

    q_t = q_ref[0].T
    km = km_ref[0]
    km_lane = lax.broadcasted_iota(jnp.int32, km.shape, 1)
    blk = lax.broadcasted_iota(jnp.int32, (nblk, tq), 0)
    col = lax.broadcasted_iota(jnp.int32, (nblk, tq), 1)
    own = (qt * tq + col) >> shift
    q_hi, q_lo = _split_bf16(q_t)
    qs = q_t * (HEAD_DIM ** -0.5 * math.log2(math.e))
    pad = jnp.zeros((HEAD_DIM - nblk, tq), F32)
    for e in range(2):
        k_hi, k_lo = _split_bf16(jnp.where((km_lane >> 6) == e, km, 0.0))
        gate = _dot(k_hi, q_hi) + _dot(k_lo, q_hi) + _dot(k_hi, q_lo)
        sel = _top3_rows(gate, blk, own)
        bias = jnp.where(sel | (blk == own), 0.0, NEG)
        if e == 0:
            qa = jnp.concatenate([qs[0:HEAD_DIM], bias, pad], axis=0)
        else:
            qa = jnp.concatenate([bias, pad, qs[HEAD_DIM:2 * HEAD_DIM]], axis=0)
        qaug[e] = qa.astype(BF16)

    acc[...] = jnp.zeros_like(acc)

    def step(j, carry, causal):
        off = pl.multiple_of(j * tk, tk)
        new = []
        for e in range(2):
            m, l = carry[2 * e], carry[2 * e + 1]
            s = _dot(kaug[e, pl.ds(off, tk), :], qaug[e])
            if causal:
                r = lax.broadcasted_iota(jnp.int32, s.shape, 0)
                c = lax.broadcasted_iota(jnp.int32, s.shape, 1)
                s = jnp.where(r <= c, s, NEG)
            m_new = jnp.maximum(m, jnp.max(s, axis=0, keepdims=True))
            alpha = jnp.exp2(m - m_new)
            p = jnp.exp2(s - m_new)
            l_new = alpha * l + jnp.sum(p, axis=0, keepdims=True)
            pv = _dot(vt_ref[:, pl.ds(off, tk)], p.astype(BF16))
            acc[e] = alpha * acc[e] + pv
            new += [m_new, l_new]
        return tuple(new)

    m0 = jnp.full((1, tq), -1e38, F32)
    l0 = jnp.zeros((1, tq), F32)
    carry = lax.fori_loop(0, qt, functools.partial(step, causal=False), (m0, l0, m0, l0))
    _, l_a, _, l_b = step(qt, carry, causal=True)
    row = lax.broadcasted_iota(jnp.int32, (2 * HEAD_DIM, tq), 0)
    o_t = jnp.where(row < HEAD_DIM, acc[0] / l_a, acc[1] / l_b)
    o_ref[0] = o_t.T.astype(BF16)


def _attention(q3, kbf3, v_t, km3):
    nb, t, w = q3.shape
    nblk = km3.shape[1]
    tq = min(ATT_TILE, t)
    assert nblk <= HEAD_DIM and t % tq == 0
    return pl.pallas_call(
        _attn_kernel,
        grid=(nb, w // LANES, t // tq),
        in_specs=[
            pl.BlockSpec((1, tq, LANES), lambda b, hp, i: (b, i, hp)),
            pl.BlockSpec((1, t, LANES), lambda b, hp, i: (b, 0, hp)),
            pl.BlockSpec((LANES, t), lambda b, hp, i: (hp, b)),
            pl.BlockSpec((1, nblk, LANES), lambda b, hp, i: (b, 0, hp)),
        ],
        out_specs=pl.BlockSpec((1, tq, LANES), lambda b, hp, i: (b, i, hp)),
        out_shape=jax.ShapeDtypeStruct((nb, t, w), BF16),
        scratch_shapes=[pltpu.VMEM((2, t, LANES), BF16),
                        pltpu.VMEM((2, LANES, tq), BF16),
                        pltpu.VMEM((2, LANES, tq), F32)],
        compiler_params=_cparams(("arbitrary", "arbitrary", "arbitrary")),
        name="moba_attention",
    )(q3, kbf3, v_t, km3)


def _merge_kernel(x_ref, ya_ref, yb_ref, yc_ref, gs_ref, wa_ref, wb_ref, wc_ref, wo_ref, gpost_ref, o_ref):
    d = x_ref.shape[1]
    merged = (gs_ref[:, 0:d].astype(F32) * _dot(ya_ref[...], wa_ref[...])
              + gs_ref[:, d:2 * d].astype(F32) * _dot(yb_ref[...], wb_ref[...])
              + gs_ref[:, 2 * d:3 * d].astype(F32) * _dot(yc_ref[...], wc_ref[...]))
    mix = _dot(merged.astype(BF16), wo_ref[...])
    o_ref[...] = x_ref[...] + _rmsnorm(mix, gpost_ref[...])


def _merge(x2d, ya, yb, yc, gs, l, wa, wb, wc, wo, gpost, tm):
    m, d = x2d.shape
    row = lambda i: (i, 0)
    lyr = lambda i: (l, 0, 0)
    return pl.pallas_call(
        _merge_kernel,
        grid=(m // tm,),
        in_specs=[
            pl.BlockSpec((tm, d), row),
            pl.BlockSpec((tm, 256), row), pl.BlockSpec((tm, 256), row), pl.BlockSpec((tm, 512), row),
            pl.BlockSpec((tm, 3 * d), row),
            _resident((None, 256, d), lyr), _resident((None, 256, d), lyr),
            _resident((None, 512, d), lyr), _resident((None, d, d), lyr),
            pl.BlockSpec((None, 1, d), lyr),
        ],
        out_specs=pl.BlockSpec((tm, d), row),
        out_shape=jax.ShapeDtypeStruct((m, d), F32),
        compiler_params=_cparams(("arbitrary",)),
        name="merge",
    )(x2d, ya, yb, yc, gs, wa, wb, wc, wo, gpost)


def _ffn_kernel(x_ref, gpre_ref, wup_ref, wdn_ref, gpost_ref, o_ref):
    x = x_ref[...]
    h = _rmsnorm(x, gpre_ref[...]).astype(BF16)
    d = x.shape[1]
    dff = wup_ref.shape[1]
    acc = jnp.zeros(x.shape, F32)
    for j in range(dff // d):
        a = jnp.maximum(_dot(h, wup_ref[:, j * d:(j + 1) * d]), 0.0)
        acc = acc + _dot((a * a).astype(BF16), wdn_ref[j * d:(j + 1) * d, :])
    o_ref[...] = x + _rmsnorm(acc, gpost_ref[...])


def _ffn(x2d, l, gpre, wup, wdn, gpost, tm):
    m, d = x2d.shape
    dff = wup.shape[-1]
    row = lambda i: (i, 0)
    lyr = lambda i: (l, 0, 0)
    return pl.pallas_call(
        _ffn_kernel,
        grid=(m // tm,),
        in_specs=[
            pl.BlockSpec((tm, d), row),
            pl.BlockSpec((None, 1, d), lyr),
            _resident((None, d, dff), lyr), _resident((None, dff, d), lyr),
            pl.BlockSpec((None, 1, d), lyr),
        ],
        out_specs=pl.BlockSpec((tm, d), row),
        out_shape=jax.ShapeDtypeStruct((m, d), F32),
        compiler_params=_cparams(("arbitrary",)),
        name="ffn",
    )(x2d, gpre, wup, wdn, gpost)


def _sample_pre_kernel(x_ref, gpre_ref, w_ref, cos_ref, sin_ref, lng_ref, lnb_ref, w00_ref, b0_ref,
                       rep_ref, bre_ref, bim_ref, cre_ref, cim_ref, ar_ref, ai_ref, h0r_ref, h0i_ref,
                       d_ref, wglu_ref, bglu_ref,
                       ya_ref, yb_ref, q_ref, k_ref, v_ref, gs_ref, hr_ref, hi_ref, vrow_ref):
    h = _rmsnorm(x_ref[...], gpre_ref[...]).astype(BF16)

    def seg(a, b):
        return _dot(h, w_ref[:, a:b])

    u = seg(0, 256)
    bu_re, bu_im = _ssm_expand(u, rep_ref, bre_ref, bim_ref)
    ar, ai = ar_ref[...], ai_ref[...]
    h0r, h0i = h0r_ref[...], h0i_ref[...]
    hr = ar * h0r - ai * h0i + bu_re
    hi = ar * h0i + ai * h0r + bu_im
    hr_ref[...] = hr
    hi_ref[...] = hi
    y = _ssm_readout(hr, hi, cre_ref, cim_ref)
    ya_ref[...] = _ssm_glu(y, u, d_ref, wglu_ref, bglu_ref).astype(BF16)

    vn = _layernorm(seg(512, 768), lng_ref[...], lnb_ref[...])
    vrow_ref[...] = vn
    yb_ref[...] = (seg(256, 512) * (w00_ref[...] * vn + b0_ref[...])).astype(BF16)

    cos, sin = cos_ref[...], sin_ref[...]
    q_ref[...] = _rope(seg(768, 1280), cos, sin)
    k_ref[...] = _rope(seg(1280, 1792), cos, sin)
    v_ref[...] = seg(1792, 2304)
    for j in range(3):
        gs_ref[:, j * 1024:(j + 1) * 1024] = jax.nn.sigmoid(
            seg(2304 + j * 1024, 2304 + (j + 1) * 1024)).astype(BF16)


def _sample_pre(xs, l, w_in_bf, gpre, cos_s, sin_s, lng, lnb, w00, b0, rep_s, bre, bim, cre, cim,
                ar_s, ai_s, h0r, h0i, d_skip, wglu_bf, bglu):
    ns, d = xs.shape
    in_w = w_in_bf.shape[-1]
    c2 = lambda i: (0, 0)
    lyr = lambda i: (l, 0, 0)
    full = lambda a: pl.BlockSpec(a.shape, c2)
    outs = (
        jax.ShapeDtypeStruct((ns, 256), BF16), jax.ShapeDtypeStruct((ns, 256), BF16),
        jax.ShapeDtypeStruct((ns, 512), F32), jax.ShapeDtypeStruct((ns, 512), F32),
        jax.ShapeDtypeStruct((ns, 512), F32), jax.ShapeDtypeStruct((ns, 3072), BF16),
        jax.ShapeDtypeStruct((ns * SUBLANES, LANES), F32), jax.ShapeDtypeStruct((ns * SUBLANES, LANES), F32),
        jax.ShapeDtypeStruct((ns, 256), F32),
    )
    return pl.pallas_call(
        _sample_pre_kernel,
        grid=(1,),
        in_specs=[
            full(xs),
            pl.BlockSpec((None, 1, d), lyr),
            pl.BlockSpec((None, d, in_w), lyr),
            full(cos_s), full(sin_s),
            pl.BlockSpec((None, 1, 256), lyr), pl.BlockSpec((None, 1, 256), lyr),
            pl.BlockSpec((None, 1, 256), lyr), pl.BlockSpec((None, 1, 256), lyr),
            full(rep_s),
            pl.BlockSpec((None, 256, LANES), lyr), pl.BlockSpec((None, 256, LANES), lyr),
            pl.BlockSpec((None, LANES, 256), lyr), pl.BlockSpec((None, LANES, 256), lyr),
            pl.BlockSpec((None, ns * SUBLANES, LANES), lyr), pl.BlockSpec((None, ns * SUBLANES, LANES), lyr),
            pl.BlockSpec((None, ns * SUBLANES, LANES), lyr), pl.BlockSpec((None, ns * SUBLANES, LANES), lyr),
            pl.BlockSpec((None, 1, 256), lyr),
            pl.BlockSpec((None, 256, 256), lyr),
            pl.BlockSpec((None, 1, 256), lyr),
        ],
        out_specs=[pl.BlockSpec(o.shape, c2) for o in outs],
        out_shape=outs,
        compiler_params=_cparams(("arbitrary",)),
        name="sample_pre",
    )(xs, gpre, w_in_bf, cos_s, sin_s, lng, lnb, w00, b0, rep_s, bre, bim, cre, cim,
      ar_s, ai_s, h0r, h0i, d_skip, wglu_bf, bglu)


def _cache_select_kernel(pt_ref, *refs, nblk):
    del pt_ref
    pages, q_ref, sel_ref, kmt = refs[:-3], refs[-3], refs[-2], refs[-1]
    j = pl.program_id(1)
    per_blk = MOBA_BLOCK // PAGE_SIZE
    blk_per_step = len(pages) // per_blk
    rows = kmt.shape[0]

    @pl.when(j == 0)
    def _():
        kmt[...] = jnp.zeros_like(kmt)

    lane = lax.broadcasted_iota(jnp.int32, kmt.shape, 1)
    acc = kmt[...]
    for i in range(blk_per_step):
        x = pages[per_blk * i][...].reshape(rows, PAGE_SIZE)
        for jj in range(1, per_blk):
            x = x + pages[per_blk * i + jj][...].reshape(rows, PAGE_SIZE)
        col = jnp.sum(x, axis=-1, keepdims=True) * (1.0 / MOBA_BLOCK)
        acc = jnp.where(lane == j * blk_per_step + i, col, acc)
    kmt[...] = acc

    @pl.when(j == pl.num_programs(1) - 1)
    def _():
        n_heads = sel_ref.shape[1]
        q = q_ref[0]
        hrow = lax.broadcasted_iota(jnp.int32, (n_heads, rows), 0)
        hlane = lax.broadcasted_iota(jnp.int32, (n_heads, rows), 1)
        qbd = jnp.where((hlane >> 6) == hrow, jnp.broadcast_to(q, (n_heads, rows)), 0.0)
        gate = jnp.dot(qbd, acc, preferred_element_type=F32, precision=lax.Precision.HIGHEST)
        idx = lax.broadcasted_iota(jnp.int32, gate.shape, 1)
        big = jnp.int32(1 << 20)
        g = jnp.where(idx < nblk, gate, -jnp.inf)
        out = jnp.zeros(gate.shape, jnp.int32)
        for r in range(MOBA_TOPK):
            m = jnp.max(g, axis=-1, keepdims=True)
            first = jnp.min(jnp.where(g == m, idx, big), axis=-1, keepdims=True)
            out = jnp.where(idx == r, first, out)
            g = jnp.where(idx == first, -jnp.inf, g)
        sel_ref[0] = out


def _cache_select(cache_kt, page_table, q_s, l):
    ns, n_pages = page_table.shape
    n_heads, hd, psz = cache_kt.shape[2:]
    w = n_heads * hd
    pp = min(PAGES_PER_STEP, n_pages)
    nblk = n_pages * psz // MOBA_BLOCK
    assert nblk <= LANES and nblk >= MOBA_TOPK

    def page_spec(i):
        return pl.BlockSpec((None, None, n_heads, hd, psz),
                            lambda b, j, pt: (l, pt[b, j * pp + i], 0, 0, 0))

    grid_spec = pltpu.PrefetchScalarGridSpec(
        num_scalar_prefetch=1,
        grid=(ns, n_pages // pp),
        in_specs=[page_spec(i) for i in range(pp)] + [pl.BlockSpec((1, 1, w), lambda b, j, pt: (b, 0, 0))],
        out_specs=pl.BlockSpec((1, n_heads, LANES), lambda b, j, pt: (b, 0, 0)),
        scratch_shapes=[pltpu.VMEM((w, LANES), F32)],
    )
    return pl.pallas_call(
        functools.partial(_cache_select_kernel, nblk=nblk),
        grid_spec=grid_spec,
        out_shape=jax.ShapeDtypeStruct((ns, n_heads, LANES), jnp.int32),
        compiler_params=_cparams(("arbitrary", "arbitrary")),
        name="cache_select",
    )(page_table, *([cache_kt] * pp), q_s.reshape(ns, 1, w))


def _sample_attn_kernel(pt_ref, sel_ref, q_ref, kn_ref, vn_ref, ck_ref, cv_ref, o_ref,
                        kbuf, vbuf, sem, *, layer, n_heads):
    b = pl.program_id(0)
    ns = pl.num_programs(0)
    per_blk = MOBA_BLOCK // PAGE_SIZE
    slot = b % 2

    def copies(seq, sl):
        out = []
        for h in range(n_heads):
            rows = pl.ds(h * HEAD_DIM, HEAD_DIM)
            for r in range(MOBA_TOPK):
                blk = sel_ref[(seq * n_heads + h) * MOBA_TOPK + r]
                for j in range(per_blk):
                    page = pt_ref[seq, blk * per_blk + j]
                    cols = pl.ds((r * per_blk + j) * PAGE_SIZE, PAGE_SIZE)
                    out.append(pltpu.make_async_copy(ck_ref.at[layer, page, h],
                                                     kbuf.at[sl, rows, cols], sem.at[0, sl]))
                    out.append(pltpu.make_async_copy(cv_ref.at[layer, page, h],
                                                     vbuf.at[sl, rows, cols], sem.at[1, sl]))
        return out

    @pl.when(b == 0)
    def _():
        for c in copies(0, 0):
            c.start()

    @pl.when(b + 1 < ns)
    def _():
        for c in copies(b + 1, 1 - slot):
            c.start()

    for c in copies(b, slot):
        c.wait()

    w = n_heads * HEAD_DIM
    hi = lax.Precision.HIGHEST
    kb = kbuf[slot]
    vb = vbuf[slot]
    q = q_ref[0] * (HEAD_DIM ** -0.5)
    hrow = lax.broadcasted_iota(jnp.int32, (n_heads, w), 0)
    hlane = lax.broadcasted_iota(jnp.int32, (n_heads, w), 1)
    mine = (hlane >> 6) == hrow
    qbd = jnp.where(mine, jnp.broadcast_to(q, (n_heads, w)), 0.0)
    s = jnp.dot(qbd, kb, preferred_element_type=F32, precision=hi)
    s_new = jnp.sum(qbd * kn_ref[0], axis=-1, keepdims=True)
    m = jnp.maximum(jnp.max(s, axis=-1, keepdims=True), s_new)
    p = jnp.exp(s - m)
    p_new = jnp.exp(s_new - m)
    den = jnp.sum(p, axis=-1, keepdims=True) + p_new
    o_all = _dot_nt(p, vb, precision=hi) + p_new * vn_ref[0]
    o = jnp.sum(jnp.where(mine, o_all / den, 0.0), axis=0, keepdims=True)
    o_ref[0] = o.astype(o_ref.dtype)


def _sample_attention(q_s, k_s, v_s, cache_kt, cache_vt, page_table, sel_flat, l):
    ns, w = q_s.shape
    n_heads = w // HEAD_DIM
    cols = MOBA_TOPK * MOBA_BLOCK
    q3, k3, v3 = (a.reshape(ns, 1, w) for a in (q_s, k_s, v_s))
    row = lambda b, pt, sel: (b, 0, 0)
    grid_spec = pltpu.PrefetchScalarGridSpec(
        num_scalar_prefetch=2,
        grid=(ns,),
        in_specs=[pl.BlockSpec((1, 1, w), row), pl.BlockSpec((1, 1, w), row), pl.BlockSpec((1, 1, w), row),
                  pl.BlockSpec(memory_space=pl.ANY), pl.BlockSpec(memory_space=pl.ANY)],
        out_specs=pl.BlockSpec((1, 1, w), row),
        scratch_shapes=[pltpu.VMEM((2, w, cols), F32),
                        pltpu.VMEM((2, w, cols), F32),
                        pltpu.SemaphoreType.DMA((2, 2))],
    )
    out = pl.pallas_call(
        functools.partial(_sample_attn_kernel, layer=l, n_heads=n_heads),
        grid_spec=grid_spec,
        out_shape=jax.ShapeDtypeStruct((ns, 1, w), BF16),
        compiler_params=_cparams(("arbitrary",)),
        name="sample_attention",
    )(page_table, sel_flat, q3, k3, v3, cache_kt, cache_vt)
    return out.reshape(ns, w)


def _rope_tables(pos, n_heads):
    half = HEAD_DIM // 2
    inv_freq = ROPE_THETA ** (-jnp.arange(half, dtype=F32) / half)
    ang = pos.astype(F32)[:, None] * inv_freq[None, :]
    cos, sin = jnp.cos(ang), jnp.sin(ang)
    cos_t = jnp.tile(jnp.concatenate([cos, cos], axis=-1), (1, n_heads))
    sin_t = jnp.tile(jnp.concatenate([-sin, sin], axis=-1), (1, n_heads))
    return cos_t, sin_t


def _ssm_tables(a_re, a_im, log_dt, b_re, b_im, c_re, c_im):
    depth, groups, p = a_re.shape
    lam_re = jnp.minimum(a_re.astype(F32), DT_EPS_LAM)
    lam_im = a_im.astype(F32)
    dt = jnp.exp(log_dt.astype(F32))[..., None]
    mag = jnp.exp(lam_re * dt)
    ab_re = mag * jnp.cos(lam_im * dt)
    ab_im = mag * jnp.sin(lam_im * dt)
    den = lam_re * lam_re + lam_im * lam_im
    num_re = ab_re - 1.0
    f_re = (num_re * lam_re + ab_im * lam_im) / den
    f_im = (ab_im * lam_re - num_re * lam_im) / den
    br, bi = b_re.astype(F32), b_im.astype(F32)
    bb_re = f_re[..., None] * br - f_im[..., None] * bi
    bb_im = f_re[..., None] * bi + f_im[..., None] * br
    eye2 = jnp.eye(2, dtype=F32)

    def b_stack(bb):
        t = bb.transpose(0, 1, 3, 2).reshape(depth, groups // 2, 2, SSM_GROUP, p)
        t = t[:, :, :, :, None, :] * eye2[None, None, :, None, :, None]
        return t.reshape(depth, groups * SSM_GROUP, 2 * p).astype(BF16)

    def c_stack(c):
        t = c.astype(F32).reshape(depth, groups // 2, 2, SSM_GROUP, p)
        t = t[:, :, :, :, None, :] * eye2[None, None, :, None, :, None]
        t = t.transpose(0, 4, 5, 1, 2, 3)
        return t.reshape(depth, 2 * p, groups * SSM_GROUP).astype(BF16)

    lay = lambda a: a.reshape(depth, SUBLANES, LANES)
    return (lay(ab_re), lay(ab_im), b_stack(bb_re), b_stack(bb_im), c_stack(c_re), c_stack(c_im))


def _repeat_matrix(n):
    return jnp.repeat(jnp.eye(n, dtype=BF16), SUBLANES, axis=0)


def kernel(x_prompt, x_sample, cache_k, cache_v, state_ssm_re, state_ssm_im, page_table, mix_pre_g, w_in, ssm_a_re, ssm_a_im, ssm_log_dt, ssm_b_re, ssm_b_im, ssm_c_re, ssm_c_im, ssm_d, ssm_w_glu, ssm_b_glu, sgu_ln_g, sgu_ln_b, sgu_w, sgu_b, w_out_a, w_out_b, w_out_c, w_o, mix_post_g, ffn_pre_g, w_up, w_down, ffn_post_g):
    bp, tp, d = x_prompt.shape
    bs, ts, _ = x_sample.shape
    depth = w_in.shape[0]
    n_heads = cache_k.shape[3]
    n_pages = page_table.shape[1]
    past_len = n_pages * PAGE_SIZE
    groups = ssm_a_re.shape[1]
    assert ts == 1 and tp % MOBA_BLOCK == 0 and past_len % MOBA_BLOCK == 0
    assert n_pages % min(PAGES_PER_STEP, n_pages) == 0 and groups * SSM_STATE == SUBLANES * LANES

    row3 = lambda a: a.reshape(depth, 1, a.shape[-1]).astype(F32)
    w_in_bf = w_in.astype(BF16)
    wa_bf, wb_bf, wc_bf, wo_bf = (w.astype(BF16) for w in (w_out_a, w_out_b, w_out_c, w_o))
    wup_bf, wdn_bf = w_up.astype(BF16), w_down.astype(BF16)
    wglu_bf = ssm_w_glu.astype(BF16)
    gpre, gpost, fpre, fpost = row3(mix_pre_g), row3(mix_post_g), row3(ffn_pre_g), row3(ffn_post_g)
    lng, lnb, d_skip, bglu = row3(sgu_ln_g), row3(sgu_ln_b), row3(ssm_d), row3(ssm_b_glu)
    cos_p, sin_p = _rope_tables(jnp.arange(tp, dtype=jnp.int32), n_heads)
    cos_s, sin_s = _rope_tables(jnp.full((bs,), past_len, dtype=jnp.int32), n_heads)
    ar, ai, bre, bim, cre, cim = _ssm_tables(ssm_a_re, ssm_a_im, ssm_log_dt, ssm_b_re, ssm_b_im,
                                             ssm_c_re, ssm_c_im)
    sgu_dim = sgu_ln_g.shape[-1] // SGU_GROUPS
    bs2d = jnp.repeat(sgu_b.astype(F32).transpose(0, 2, 1), sgu_dim, axis=-1)
    w00 = jnp.repeat(sgu_w[:, :, 0, 0].astype(F32), sgu_dim, axis=-1).reshape(depth, 1, -1)
    b0 = bs2d[:, 0:1, :]
    cls = (jnp.arange(groups * SSM_GROUP) // (2 * SSM_GROUP))[None, :] == jnp.arange(SUBLANES)[:, None]
    bre8, bim8 = (jnp.where(cls[None, :, :, None], m[:, None], 0).astype(BF16) for m in (bre, bim))
    cre8, cim8 = (jnp.where(cls[None, :, None, :], m[:, None], 0).astype(BF16) for m in (cre, cim))
    rep_s = _repeat_matrix(bs)
    ar_s, ai_s = jnp.tile(ar, (1, bs, 1)), jnp.tile(ai, (1, bs, 1))
    h0r = state_ssm_re.astype(F32).reshape(depth, bs * SUBLANES, LANES)
    h0i = state_ssm_im.astype(F32).reshape(depth, bs * SUBLANES, LANES)
    v_cols = slice(256 + 2 * 256 + 2 * n_heads * HEAD_DIM, 256 + 2 * 256 + 3 * n_heads * HEAD_DIM)
    wvt_bf = w_in_bf[:, :, v_cols].transpose(0, 2, 1)
    cache_kt = cache_k.transpose(0, 1, 3, 4, 2)
    cache_vt = cache_v.transpose(0, 1, 3, 4, 2)

    xp = x_prompt.reshape(bp * tp, d)
    xs = x_sample.reshape(bs * ts, d)
    kp_l, vp_l, hrp_l, hip_l = [], [], [], []
    ks_l, vs_l, hrs_l, his_l, sv_l = [], [], [], [], []
    for l in range(depth):
        u, yb, q, k, v, kbf, v_t, gs, km = _inproj(xp, l, w_in_bf, wvt_bf, gpre, cos_p, sin_p, lng, lnb,
                                                   sgu_w, bs2d, tp)
        ya, hre, him = _ssm_scan(u.reshape(bp, tp, -1), l, SSM_TILE, bre8, bim8, cre8, cim8, ar, ai,
                                 d_skip, wglu_bf, bglu)
        yc = _attention(q.reshape(bp, tp, -1), kbf.reshape(bp, tp, -1), v_t,
                        km.reshape(bp, tp // MOBA_BLOCK, -1))
        xp = _merge(xp, ya.reshape(bp * tp, -1), yb, yc.reshape(bp * tp, -1), gs, l,
                    wa_bf, wb_bf, wc_bf, wo_bf, gpost, 512)
        xp = _ffn(xp, l, fpre, wup_bf, wdn_bf, fpost, 512)
        kp_l.append(k.reshape(bp, tp, n_heads, HEAD_DIM))
        vp_l.append(v.reshape(bp, tp, n_heads, HEAD_DIM))
        hrp_l.append(hre.reshape(bp, groups, SSM_STATE))
        hip_l.append(him.reshape(bp, groups, SSM_STATE))

        (ya_s, yb_s, q_s, k_s, v_s, gs_s, hr_s, hi_s, vrow) = _sample_pre(
            xs, l, w_in_bf, gpre, cos_s, sin_s, lng, lnb, w00, b0, rep_s, bre, bim, cre, cim,
            ar_s, ai_s, h0r, h0i, d_skip, wglu_bf, bglu)
        sel_s = _cache_select(cache_kt, page_table, q_s, l)
        sel_flat = sel_s[:, :, :MOBA_TOPK].reshape(-1)
        yc_s = _sample_attention(q_s, k_s, v_s, cache_kt, cache_vt, page_table, sel_flat, l)
        xs = _merge(xs, ya_s, yb_s, yc_s, gs_s, l, wa_bf, wb_bf, wc_bf, wo_bf, gpost, bs)
        xs = _ffn(xs, l, fpre, wup_bf, wdn_bf, fpost, bs)
        ks_l.append(k_s.reshape(bs, ts, n_heads, HEAD_DIM))
        vs_l.append(v_s.reshape(bs, ts, n_heads, HEAD_DIM))
        hrs_l.append(hr_s.reshape(bs, groups, SSM_STATE))
        his_l.append(hi_s.reshape(bs, groups, SSM_STATE))
        sv_l.append(vrow.reshape(bs, ts, -1))

    return (xp.reshape(bp, tp, d), xs.reshape(bs, ts, d),
            jnp.stack(kp_l), jnp.stack(vp_l), jnp.stack(hrp_l), jnp.stack(hip_l),
            jnp.stack(ks_l), jnp.stack(vs_l), jnp.stack(hrs_l), jnp.stack(his_l), jnp.stack(sv_l))
```

```python
import functools
import math

import jax
import jax.numpy as jnp
from jax import lax
from jax.experimental import pallas as pl
from jax.experimental.pallas import tpu as pltpu

F32 = jnp.float32
BF16 = jnp.bfloat16

SSM_GROUP = 16
SSM_STATE = 64
CHUNK = 128
SGU_GROUPS = 4
HEAD_DIM = 64
MOBA_BLOCK = 256
MOBA_TOPK = 3
PAGE_SIZE = 128
ROPE_THETA = 10000.0
DT_EPS_LAM = -1e-4
EPS = 1e-6
NEG = -1e30

LANES = 128
SUBLANES = 8
VMEM_LIMIT = 56 * 1024 * 1024

PAGES_PER_STEP = 16
ATT_TILE = 512
SSM_TILE = 256


def _cparams(sem):
    return pltpu.CompilerParams(dimension_semantics=sem, vmem_limit_bytes=VMEM_LIMIT)


def _resident(block_shape, index_map):
    return pl.BlockSpec(block_shape, index_map, pipeline_mode=pl.Buffered(1))


def _rmsnorm(x, g):
    return x * lax.rsqrt(jnp.mean(x * x, axis=-1, keepdims=True) + EPS) * g


def _layernorm(v, g, b):
    mu = jnp.mean(v, axis=-1, keepdims=True)
    d = v - mu
    var = jnp.mean(d * d, axis=-1, keepdims=True)
    return d * lax.rsqrt(var + EPS) * g + b


def _gelu_tanh(x):
    c = math.sqrt(2.0 / math.pi)
    return 0.5 * x * (1.0 + jnp.tanh(c * (x + 0.044715 * (x * x * x))))


def _rope(x, cos, sin_signed):
    n = x.shape[-1]
    half = HEAD_DIM // 2
    lane = lax.broadcasted_iota(jnp.int32, x.shape, 1)
    first = (lane & (HEAD_DIM - 1)) < half
    swapped = jnp.where(first, pltpu.roll(x, n - half, axis=1), pltpu.roll(x, half, axis=1))
    return x * cos + swapped * sin_signed


def _dot(a, b):
    return jnp.dot(a, b, preferred_element_type=F32)


def _dot_nt(a, b, precision=None):
    return lax.dot_general(a, b, (((1,), (1,)), ((), ())), preferred_element_type=F32,
                           precision=precision)


def _ssm_expand(u, rep_ref, bre_ref, bim_ref):
    um = _dot(rep_ref[...], u.astype(BF16))
    row = lax.broadcasted_iota(jnp.int32, um.shape, 0)
    col = lax.broadcasted_iota(jnp.int32, um.shape, 1)
    keep = (row & (SUBLANES - 1)) == (col >> 5)
    um = jnp.where(keep, um, 0.0).astype(BF16)
    return _dot(um, bre_ref[...]), _dot(um, bim_ref[...])


def _ssm_readout(h_re, h_im, cre_ref, cim_ref):
    y2 = _dot(h_re.astype(BF16), cre_ref[...]) - _dot(h_im.astype(BF16), cim_ref[...])
    row = lax.broadcasted_iota(jnp.int32, y2.shape, 0)
    col = lax.broadcasted_iota(jnp.int32, y2.shape, 1)
    keep = (row & (SUBLANES - 1)) == (col >> 5)
    y2 = jnp.where(keep, y2, 0.0)
    n = y2.shape[0] // SUBLANES
    return jnp.sum(y2.reshape(n, SUBLANES, y2.shape[1]), axis=1)


def _ssm_glu(y, u, d_ref, wglu_ref, bglu_ref):
    y = y + d_ref[...] * u
    g = _gelu_tanh(y)
    return g * jax.nn.sigmoid(_dot(g.astype(BF16), wglu_ref[...]) + bglu_ref[...])


def _rope_rows(x, cos_t, sin_t):
    half = HEAD_DIM // 2
    parts = []
    for h0 in range(0, x.shape[0], HEAD_DIM):
        parts += [x[h0 + half:h0 + HEAD_DIM], x[h0:h0 + half]]
    return x * cos_t + jnp.concatenate(parts, axis=0) * sin_t


def _inproj_kernel(x_ref, gpre_ref, w_ref, wkt_ref, wvt_ref, cos_ref, sin_ref, cost_ref, sint_ref,
                   lng_ref, lnb_ref, ws_ref, bs_ref, *rest):
    u_ref, yb_ref, q_ref, kt_ref, v_ref, kbf_ref, vt_ref, gs_ref, km_ref = rest[-9:]
    tm = x_ref.shape[0]
    h = _rmsnorm(x_ref[...], gpre_ref[...]).astype(BF16)

    def seg(a, b):
        return _dot(h, w_ref[:, a:b])

    u_ref[...] = seg(0, 256)

    u_sgu = seg(256, 512)
    vn = _layernorm(seg(512, 768), lng_ref[...], lnb_ref[...])
    r = lax.broadcasted_iota(jnp.int32, (CHUNK, CHUNK), 0)
    c = lax.broadcasted_iota(jnp.int32, (CHUNK, CHUNK), 1)
    tril = c <= r
    grp = lax.broadcasted_iota(jnp.int32, (CHUNK, 256), 1) >> 6
    wmask = [jnp.where(tril, ws_ref[g], 0.0).astype(BF16) for g in range(SGU_GROUPS)]
    for ci in range(tm // CHUNK):
        rows = slice(ci * CHUNK, (ci + 1) * CHUNK)
        vc = vn[rows]
        mixed = bs_ref[...]
        for g in range(SGU_GROUPS):
            mixed = mixed + _dot(wmask[g], jnp.where(grp == g, vc, 0.0).astype(BF16))
        yb_ref[rows, :] = (u_sgu[rows] * mixed).astype(BF16)

    cos = cos_ref[...]
    sin = sin_ref[...]
    q_ref[...] = _rope(seg(768, 1280), cos, sin)
    k = _rope(seg(1280, 1792), cos, sin)
    kbf_ref[...] = k.astype(BF16)
    km_ref[0] = jnp.mean(k, axis=0, keepdims=True)
    kt_ref[...] = _rope_rows(_dot_nt(wkt_ref[...], h), cost_ref[...], sint_ref[...])
    v_t = _dot_nt(wvt_ref[...], h)
    v_ref[...] = v_t
    vt_ref[...] = v_t.astype(BF16)
    for j in range(3):
        gs_ref[:, j * 1024:(j + 1) * 1024] = jax.nn.sigmoid(
            seg(2304 + j * 1024, 2304 + (j + 1) * 1024)).astype(BF16)


def _inproj(x2d, l, depth, w_in_bf, wkt_bf, wvt_bf, gpre, cos_t, sin_t, cos_tt, sin_tt, lng, lnb, sgu_w, bs2d,
            seq_len, kv_stacks):
    m, d = x2d.shape
    tm = MOBA_BLOCK
    n_t = seq_len // tm
    n_seq = m // seq_len
    in_w = w_in_bf.shape[-1]
    row = lambda i: (i, 0)
    lyr3 = lambda i: (l, 0, 0)
    stack = jax.ShapeDtypeStruct((depth, n_seq, 512, seq_len), F32)
    stack_spec = pl.BlockSpec((None, None, 512, tm), lambda i: (l, i // n_t, 0, i % n_t))
    outs = (
        jax.ShapeDtypeStruct((m, 256), F32),
        jax.ShapeDtypeStruct((m, 256), BF16),
        jax.ShapeDtypeStruct((m, 512), F32),
        stack,
        stack,
        jax.ShapeDtypeStruct((m, 512), BF16),
        jax.ShapeDtypeStruct((512, m), BF16),
        jax.ShapeDtypeStruct((m, 3072), BF16),
        jax.ShapeDtypeStruct((m // tm, 1, 512), F32),
    )
    in_specs = [
        pl.BlockSpec((tm, d), row),
        pl.BlockSpec((None, 1, d), lyr3),
        _resident((None, d, in_w), lyr3),
        _resident((None, 512, d), lyr3),
        _resident((None, 512, d), lyr3),
        pl.BlockSpec((tm, 512), lambda i: (i % n_t, 0)),
        pl.BlockSpec((tm, 512), lambda i: (i % n_t, 0)),
        pl.BlockSpec((512, tm), lambda i: (0, i % n_t)),
        pl.BlockSpec((512, tm), lambda i: (0, i % n_t)),
        pl.BlockSpec((None, 1, 256), lyr3),
        pl.BlockSpec((None, 1, 256), lyr3),
        pl.BlockSpec((None, SGU_GROUPS, CHUNK, CHUNK), lambda i: (l, 0, 0, 0)),
        pl.BlockSpec((None, CHUNK, 256), lyr3),
    ]
    args = [x2d, gpre, w_in_bf, wkt_bf, wvt_bf, cos_t, sin_t, cos_tt, sin_tt, lng, lnb, sgu_w, bs2d]
    aliases = {len(args): 3, len(args) + 1: 4}
    in_specs += [pl.BlockSpec(memory_space=pl.ANY)] * 2
    args += list(kv_stacks)
    return pl.pallas_call(
        _inproj_kernel,
        grid=(m // tm,),
        in_specs=in_specs,
        out_specs=[
            pl.BlockSpec((tm, 256), row), pl.BlockSpec((tm, 256), row),
            pl.BlockSpec((tm, 512), row), stack_spec, stack_spec,
            pl.BlockSpec((tm, 512), row), pl.BlockSpec((512, tm), lambda i: (0, i)),
            pl.BlockSpec((tm, 3072), row),
            pl.BlockSpec((1, 1, 512), lambda i: (i, 0, 0)),
        ],
        out_shape=outs,
        input_output_aliases=aliases,
        compiler_params=_cparams(("arbitrary",)),
        name="inproj",
    )(*args)


def _ssm_kernel(u_ref, bre_ref, bim_ref, cre_ref, cim_ref, ar_ref, ai_ref,
                d_ref, wglu_ref, bglu_ref, ya_ref, hre_ref, him_ref, sre, sim):
    nb, tt = u_ref.shape[0], u_ref.shape[1]
    step = pl.program_id(0)

    @pl.when(step == 0)
    def _():
        hre_ref[...] = jnp.zeros_like(hre_ref)
        him_ref[...] = jnp.zeros_like(him_ref)

    for b in range(nb):
        ub = u_ref[b].astype(BF16)
        for j in range(SUBLANES):
            rows_j = pl.ds(j, tt, stride=SUBLANES)
            sre[b, rows_j, :] = _dot(ub, bre_ref[j])
            sim[b, rows_j, :] = _dot(ub, bim_ref[j])

    ar = ar_ref[...]
    ai = ai_ref[...]

    def body(t, carry):
        off = pl.multiple_of(t * SUBLANES, SUBLANES)
        new = []
        for b in range(nb):
            hr, hi = carry[2 * b], carry[2 * b + 1]
            nr = ar * hr - ai * hi + sre[b, pl.ds(off, SUBLANES), :]
            ni = ar * hi + ai * hr + sim[b, pl.ds(off, SUBLANES), :]
            sre[b, pl.ds(off, SUBLANES), :] = nr
            sim[b, pl.ds(off, SUBLANES), :] = ni
            new += [nr, ni]
        return tuple(new)

    init = []
    for b in range(nb):
        init += [hre_ref[b], him_ref[b]]
    fin = lax.fori_loop(0, tt, body, tuple(init), unroll=8)
    for b in range(nb):
        hre_ref[b] = fin[2 * b]
        him_ref[b] = fin[2 * b + 1]

    for b in range(nb):
        y = jnp.zeros((tt, u_ref.shape[2]), F32)
        for j in range(SUBLANES):
            rows_j = pl.ds(j, tt, stride=SUBLANES)
            y = y + _dot(sre[b, rows_j, :].astype(BF16), cre_ref[j])
            y = y - _dot(sim[b, rows_j, :].astype(BF16), cim_ref[j])
        ya_ref[b] = _ssm_glu(y, u_ref[b], d_ref, wglu_ref, bglu_ref).astype(BF16)


def _ssm_scan(u3, l, tt, bre8, bim8, cre8, cim8, ar, ai, d_skip, wglu_bf, bglu):
    nb, t, w = u3.shape
    lyr3 = lambda i: (l, 0, 0)
    lyr4 = lambda i: (l, 0, 0, 0)
    return pl.pallas_call(
        _ssm_kernel,
        grid=(t // tt,),
        in_specs=[
            pl.BlockSpec((nb, tt, w), lambda i: (0, i, 0)),
            pl.BlockSpec((None, SUBLANES, 256, LANES), lyr4), pl.BlockSpec((None, SUBLANES, 256, LANES), lyr4),
            pl.BlockSpec((None, SUBLANES, LANES, 256), lyr4), pl.BlockSpec((None, SUBLANES, LANES, 256), lyr4),
            pl.BlockSpec((None, SUBLANES, LANES), lyr3), pl.BlockSpec((None, SUBLANES, LANES), lyr3),
            pl.BlockSpec((None, 1, 256), lyr3),
            pl.BlockSpec((None, 256, 256), lyr3),
            pl.BlockSpec((None, 1, 256), lyr3),
        ],
        out_specs=[
            pl.BlockSpec((nb, tt, w), lambda i: (0, i, 0)),
            pl.BlockSpec((nb, SUBLANES, LANES), lambda i: (0, 0, 0)),
            pl.BlockSpec((nb, SUBLANES, LANES), lambda i: (0, 0, 0)),
        ],
        out_shape=(
            jax.ShapeDtypeStruct((nb, t, w), BF16),
            jax.ShapeDtypeStruct((nb, SUBLANES, LANES), F32),
            jax.ShapeDtypeStruct((nb, SUBLANES, LANES), F32),
        ),
        scratch_shapes=[pltpu.VMEM((nb, SUBLANES * tt, LANES), F32),
                        pltpu.VMEM((nb, SUBLANES * tt, LANES), F32)],
        compiler_params=_cparams(("arbitrary",)),
        name="ssm_scan",
    )(u3, bre8, bim8, cre8, cim8, ar, ai, d_skip, wglu_bf, bglu)


def _top3_rows(gate, blk, own):
    big = jnp.int32(1 << 20)
    g = jnp.where(blk < own, gate, NEG)
    sel = jnp.zeros(gate.shape, jnp.bool_)
    for _ in range(MOBA_TOPK):
        m = jnp.max(g, axis=0, keepdims=True)
        first = jnp.min(jnp.where(g == m, blk, big), axis=0, keepdims=True)
        pick = blk == first
        sel = sel | (pick & (blk < own))
        g = jnp.where(pick, -jnp.inf, g)
    return sel


def _split_bf16(x):
    hi = x.astype(BF16)
    return hi, (x - hi.astype(F32)).astype(BF16)


def _attn_kernel(q_ref, k_ref, vt_ref, km_ref, o_ref, kaug, qaug, acc, sbuf0, sbuf1):
    qt = pl.program_id(2)
    tq = q_ref.shape[1]
    tk = tq
    t = k_ref.shape[1]
    nblk = km_ref.shape[1]
    shift = MOBA_BLOCK.bit_length() - 1

    @pl.when(qt == 0)
    def _():
        def fill(i, carry):
            off = pl.multiple_of(i * tk, tk)
            kk = k_ref[0, pl.ds(off, tk), :].astype(F32)
            lane = lax.broadcasted_iota(jnp.int32, kk.shape, 1)
            blk = (lax.broadcasted_iota(jnp.int32, kk.shape, 0) + off) >> shift
            hot_a = jnp.where(lane - HEAD_DIM == blk, 1.0, 0.0)
            hot_b = jnp.where(lane == blk, 1.0, 0.0)
            kaug[0, pl.ds(off, tk), :] = jnp.where(lane < HEAD_DIM, kk, hot_a).astype(BF16)
            kaug[1, pl.ds(off, tk), :] = jnp.where(lane >= HEAD_DIM, kk, hot_b).astype(BF16)
            return carry
        lax.fori_loop(0, t // tk, fill, 0)

    q_t = q_ref[0].T
    km = km_ref[0]
    km_lane = lax.broadcasted_iota(jnp.int32, km.shape, 1)
    blk = lax.broadcasted_iota(jnp.int32, (nblk, tq), 0)
    col = lax.broadcasted_iota(jnp.int32, (nblk, tq), 1)
    own = (qt * tq + col) >> shift
    q_hi, q_lo = _split_bf16(q_t)
    qs = q_t * (HEAD_DIM ** -0.5 * math.log2(math.e))
    pad = jnp.zeros((HEAD_DIM - nblk, tq), F32)
    for e in range(2):
        k_hi, k_lo = _split_bf16(jnp.where((km_lane >> 6) == e, km, 0.0))
        gate = _dot(k_hi, q_hi) + _dot(k_lo, q_hi) + _dot(k_hi, q_lo)
        sel = _top3_rows(gate, blk, own)
        bias = jnp.where(sel | (blk == own), 0.0, NEG)
        if e == 0:
            qa = jnp.concatenate([qs[0:HEAD_DIM], bias, pad], axis=0)
        else:
            qa = jnp.concatenate([bias, pad, qs[HEAD_DIM:2 * HEAD_DIM]], axis=0)
        qaug[e] = qa.astype(BF16)

    acc[...] = jnp.zeros_like(acc)

    def scores(j, dst):
        off = pl.multiple_of(j * tk, tk)
        for e in range(2):
            dst[e] = _dot(kaug[e, pl.ds(off, tk), :], qaug[e])

    def step(j, carry, cur, nxt, last=False):
        if not last:
            scores(j + 1, nxt)
        off = pl.multiple_of(j * tk, tk)
        new = []
        for e in range(2):
            m, l = carry[2 * e], carry[2 * e + 1]
            s = cur[e]
            if last:
                r = lax.broadcasted_iota(jnp.int32, s.shape, 0)
                c = lax.broadcasted_iota(jnp.int32, s.shape, 1)
                s = jnp.where(r <= c, s, NEG)
            m_new = jnp.maximum(m, jnp.max(s, axis=0, keepdims=True))
            alpha = jnp.exp2(m - m_new)
            p = jnp.exp2(s - m_new)
            l_new = alpha * l + jnp.sum(p, axis=0, keepdims=True)
            pv = _dot(vt_ref[:, pl.ds(off, tk)], p.astype(BF16))
            acc[e] = alpha * acc[e] + pv
            new += [m_new, l_new]
        return tuple(new)

    m0 = jnp.full((1, tq), -1e38, F32)
    l0 = jnp.zeros((1, tq), F32)
    scores(0, sbuf0)

    def pair(i, carry):
        carry = step(2 * i, carry, sbuf0, sbuf1)
        return step(2 * i + 1, carry, sbuf1, sbuf0)

    carry = lax.fori_loop(0, qt // 2, pair, (m0, l0, m0, l0))

    def tail_even(carry):
        return step(qt, carry, sbuf0, sbuf1, last=True)

    def tail_odd(carry):
        carry = step(qt - 1, carry, sbuf0, sbuf1)
        return step(qt, carry, sbuf1, sbuf0, last=True)

    _, l_a, _, l_b = lax.cond(qt % 2 == 0, tail_even, tail_odd, carry)
    row = lax.broadcasted_iota(jnp.int32, (2 * HEAD_DIM, tq), 0)
    o_t = jnp.where(row < HEAD_DIM, acc[0] / l_a, acc[1] / l_b)
    o_ref[0] = o_t.T.astype(BF16)


def _attention(q3, kbf3, v_t, km3):
    nb, t, w = q3.shape
    nblk = km3.shape[1]
    tq = min(ATT_TILE, t)
    assert nblk <= HEAD_DIM and t % tq == 0
    return pl.pallas_call(
        _attn_kernel,
        grid=(nb, w // LANES, t // tq),
        in_specs=[
            pl.BlockSpec((1, tq, LANES), lambda b, hp, i: (b, i, hp)),
            pl.BlockSpec((1, t, LANES), lambda b, hp, i: (b, 0, hp)),
            pl.BlockSpec((LANES, t), lambda b, hp, i: (hp, b)),
            pl.BlockSpec((1, nblk, LANES), lambda b, hp, i: (b, 0, hp)),
        ],
        out_specs=pl.BlockSpec((1, tq, LANES), lambda b, hp, i: (b, i, hp)),
        out_shape=jax.ShapeDtypeStruct((nb, t, w), BF16),
        scratch_shapes=[pltpu.VMEM((2, t, LANES), BF16),
                        pltpu.VMEM((2, LANES, tq), BF16),
                        pltpu.VMEM((2, LANES, tq), F32),
                        pltpu.VMEM((2, tq, tq), F32),
                        pltpu.VMEM((2, tq, tq), F32)],
        compiler_params=_cparams(("arbitrary", "arbitrary", "arbitrary")),
        name="moba_attention",
    )(q3, kbf3, v_t, km3)


def _merge_kernel(x_ref, ya_ref, yb_ref, yc_ref, gs_ref, wa_ref, wb_ref, wc_ref, wo_ref, gpost_ref, o_ref):
    d = x_ref.shape[1]
    merged = (gs_ref[:, 0:d].astype(F32) * _dot(ya_ref[...], wa_ref[...])
              + gs_ref[:, d:2 * d].astype(F32) * _dot(yb_ref[...], wb_ref[...])
              + gs_ref[:, 2 * d:3 * d].astype(F32) * _dot(yc_ref[...], wc_ref[...]))
    mix = _dot(merged.astype(BF16), wo_ref[...])
    o_ref[...] = x_ref[...] + _rmsnorm(mix, gpost_ref[...])


def _merge(x2d, ya, yb, yc, gs, l, wa, wb, wc, wo, gpost, tm):
    m, d = x2d.shape
    row = lambda i: (i, 0)
    lyr = lambda i: (l, 0, 0)
    return pl.pallas_call(
        _merge_kernel,
        grid=(m // tm,),
        in_specs=[
            pl.BlockSpec((tm, d), row),
            pl.BlockSpec((tm, 256), row), pl.BlockSpec((tm, 256), row), pl.BlockSpec((tm, 512), row),
            pl.BlockSpec((tm, 3 * d), row),
            _resident((None, 256, d), lyr), _resident((None, 256, d), lyr),
            _resident((None, 512, d), lyr), _resident((None, d, d), lyr),
            pl.BlockSpec((None, 1, d), lyr),
        ],
        out_specs=pl.BlockSpec((tm, d), row),
        out_shape=jax.ShapeDtypeStruct((m, d), F32),
        compiler_params=_cparams(("arbitrary",)),
        name="merge",
    )(x2d, ya, yb, yc, gs, wa, wb, wc, wo, gpost)


def _ffn_kernel(x_ref, gpre_ref, wup_ref, wdn_ref, gpost_ref, o_ref):
    x = x_ref[...]
    h = _rmsnorm(x, gpre_ref[...]).astype(BF16)
    d = x.shape[1]
    dff = wup_ref.shape[1]
    acc = jnp.zeros(x.shape, F32)
    for j in range(dff // d):
        a = jnp.maximum(_dot(h, wup_ref[:, j * d:(j + 1) * d]), 0.0)
        acc = acc + _dot((a * a).astype(BF16), wdn_ref[j * d:(j + 1) * d, :])
    o_ref[...] = x + _rmsnorm(acc, gpost_ref[...])


def _ffn(x2d, l, gpre, wup, wdn, gpost, tm):
    m, d = x2d.shape
    dff = wup.shape[-1]
    row = lambda i: (i, 0)
    lyr = lambda i: (l, 0, 0)
    return pl.pallas_call(
        _ffn_kernel,
        grid=(m // tm,),
        in_specs=[
            pl.BlockSpec((tm, d), row),
            pl.BlockSpec((None, 1, d), lyr),
            _resident((None, d, dff), lyr), _resident((None, dff, d), lyr),
            pl.BlockSpec((None, 1, d), lyr),
        ],
        out_specs=pl.BlockSpec((tm, d), row),
        out_shape=jax.ShapeDtypeStruct((m, d), F32),
        compiler_params=_cparams(("arbitrary",)),
        name="ffn",
    )(x2d, gpre, wup, wdn, gpost)


def _sample_pre_kernel(x_ref, gpre_ref, w_ref, cos_ref, sin_ref, lng_ref, lnb_ref, w00_ref, b0_ref,
                       rep_ref, bre_ref, bim_ref, cre_ref, cim_ref, ar_ref, ai_ref, h0r_ref, h0i_ref,
                       d_ref, wglu_ref, bglu_ref,
                       ya_ref, yb_ref, q_ref, k_ref, v_ref, gs_ref, hr_ref, hi_ref, vrow_ref):
    h = _rmsnorm(x_ref[...], gpre_ref[...]).astype(BF16)

    def seg(a, b):
        return _dot(h, w_ref[:, a:b])

    u = seg(0, 256)
    bu_re, bu_im = _ssm_expand(u, rep_ref, bre_ref, bim_ref)
    ar, ai = ar_ref[...], ai_ref[...]
    h0r, h0i = h0r_ref[...], h0i_ref[...]
    hr = ar * h0r - ai * h0i + bu_re
    hi = ar * h0i + ai * h0r + bu_im
    hr_ref[...] = hr
    hi_ref[...] = hi
    y = _ssm_readout(hr, hi, cre_ref, cim_ref)
    ya_ref[...] = _ssm_glu(y, u, d_ref, wglu_ref, bglu_ref).astype(BF16)

    vn = _layernorm(seg(512, 768), lng_ref[...], lnb_ref[...])
    vrow_ref[...] = vn
    yb_ref[...] = (seg(256, 512) * (w00_ref[...] * vn + b0_ref[...])).astype(BF16)

    cos, sin = cos_ref[...], sin_ref[...]
    q_ref[...] = _rope(seg(768, 1280), cos, sin)
    k_ref[...] = _rope(seg(1280, 1792), cos, sin)
    v_ref[...] = seg(1792, 2304)
    for j in range(3):
        gs_ref[:, j * 1024:(j + 1) * 1024] = jax.nn.sigmoid(
            seg(2304 + j * 1024, 2304 + (j + 1) * 1024)).astype(BF16)


def _sample_pre(xs, l, w_in_bf, gpre, cos_s, sin_s, lng, lnb, w00, b0, rep_s, bre, bim, cre, cim,
                ar_s, ai_s, h0r, h0i, d_skip, wglu_bf, bglu):
    ns, d = xs.shape
    in_w = w_in_bf.shape[-1]
    c2 = lambda i: (0, 0)
    lyr = lambda i: (l, 0, 0)
    full = lambda a: pl.BlockSpec(a.shape, c2)
    outs = (
        jax.ShapeDtypeStruct((ns, 256), BF16), jax.ShapeDtypeStruct((ns, 256), BF16),
        jax.ShapeDtypeStruct((ns, 512), F32), jax.ShapeDtypeStruct((ns, 512), F32),
        jax.ShapeDtypeStruct((ns, 512), F32), jax.ShapeDtypeStruct((ns, 3072), BF16),
        jax.ShapeDtypeStruct((ns * SUBLANES, LANES), F32), jax.ShapeDtypeStruct((ns * SUBLANES, LANES), F32),
        jax.ShapeDtypeStruct((ns, 256), F32),
    )
    return pl.pallas_call(
        _sample_pre_kernel,
        grid=(1,),
        in_specs=[
            full(xs),
            pl.BlockSpec((None, 1, d), lyr),
            pl.BlockSpec((None, d, in_w), lyr),
            full(cos_s), full(sin_s),
            pl.BlockSpec((None, 1, 256), lyr), pl.BlockSpec((None, 1, 256), lyr),
            pl.BlockSpec((None, 1, 256), lyr), pl.BlockSpec((None, 1, 256), lyr),
            full(rep_s),
            pl.BlockSpec((None, 256, LANES), lyr), pl.BlockSpec((None, 256, LANES), lyr),
            pl.BlockSpec((None, LANES, 256), lyr), pl.BlockSpec((None, LANES, 256), lyr),
            pl.BlockSpec((None, ns * SUBLANES, LANES), lyr), pl.BlockSpec((None, ns * SUBLANES, LANES), lyr),
            pl.BlockSpec((None, ns * SUBLANES, LANES), lyr), pl.BlockSpec((None, ns * SUBLANES, LANES), lyr),
            pl.BlockSpec((None, 1, 256), lyr),
            pl.BlockSpec((None, 256, 256), lyr),
            pl.BlockSpec((None, 1, 256), lyr),
        ],
        out_specs=[pl.BlockSpec(o.shape, c2) for o in outs],
        out_shape=outs,
        compiler_params=_cparams(("arbitrary",)),
        name="sample_pre",
    )(xs, gpre, w_in_bf, cos_s, sin_s, lng, lnb, w00, b0, rep_s, bre, bim, cre, cim,
      ar_s, ai_s, h0r, h0i, d_skip, wglu_bf, bglu)


def _cache_select_kernel(pt_ref, *refs, nblk):
    del pt_ref
    pages, q_ref, sel_ref, kmt = refs[:-3], refs[-3], refs[-2], refs[-1]
    j = pl.program_id(1)
    per_blk = MOBA_BLOCK // PAGE_SIZE
    blk_per_step = len(pages) // per_blk
    rows = kmt.shape[0]

    @pl.when(j == 0)
    def _():
        kmt[...] = jnp.zeros_like(kmt)

    lane = lax.broadcasted_iota(jnp.int32, kmt.shape, 1)
    ones = jnp.ones((PAGE_SIZE, LANES), BF16)
    acc = kmt[...]
    for i in range(blk_per_step):
        x = pages[per_blk * i][...].reshape(rows, PAGE_SIZE)
        for jj in range(1, per_blk):
            x = x + pages[per_blk * i + jj][...].reshape(rows, PAGE_SIZE)
        x_hi, x_lo = _split_bf16(x)
        tot = _dot(x_hi, ones) + _dot(x_lo, ones)
        acc = jnp.where(lane == j * blk_per_step + i, tot * (1.0 / MOBA_BLOCK), acc)
    kmt[...] = acc

    @pl.when(j == pl.num_programs(1) - 1)
    def _():
        n_heads = sel_ref.shape[1]
        q = q_ref[0]
        hrow = lax.broadcasted_iota(jnp.int32, (n_heads, rows), 0)
        hlane = lax.broadcasted_iota(jnp.int32, (n_heads, rows), 1)
        qbd = jnp.where((hlane >> 6) == hrow, jnp.broadcast_to(q, (n_heads, rows)), 0.0)
        gate = jnp.dot(qbd, acc, preferred_element_type=F32, precision=lax.Precision.HIGHEST)
        idx = lax.broadcasted_iota(jnp.int32, gate.shape, 1)
        big = jnp.int32(1 << 20)
        g = jnp.where(idx < nblk, gate, -jnp.inf)
        out = jnp.zeros(gate.shape, jnp.int32)
        for r in range(MOBA_TOPK):
            m = jnp.max(g, axis=-1, keepdims=True)
            first = jnp.min(jnp.where(g == m, idx, big), axis=-1, keepdims=True)
            out = jnp.where(idx == r, first, out)
            g = jnp.where(idx == first, -jnp.inf, g)
        sel_ref[0] = out


def _cache_select(cache_kt, page_table, q_s, l):
    ns, n_pages = page_table.shape
    n_heads, hd, psz = cache_kt.shape[2:]
    w = n_heads * hd
    pp = min(PAGES_PER_STEP, n_pages)
    nblk = n_pages * psz // MOBA_BLOCK
    assert nblk <= LANES and nblk >= MOBA_TOPK

    def page_spec(i):
        return pl.BlockSpec((None, None, n_heads, hd, psz),
                            lambda b, j, pt: (l, pt[b, j * pp + i], 0, 0, 0))

    grid_spec = pltpu.PrefetchScalarGridSpec(
        num_scalar_prefetch=1,
        grid=(ns, n_pages // pp),
        in_specs=[page_spec(i) for i in range(pp)] + [pl.BlockSpec((1, 1, w), lambda b, j, pt: (b, 0, 0))],
        out_specs=pl.BlockSpec((1, n_heads, LANES), lambda b, j, pt: (b, 0, 0)),
        scratch_shapes=[pltpu.VMEM((w, LANES), F32)],
    )
    return pl.pallas_call(
        functools.partial(_cache_select_kernel, nblk=nblk),
        grid_spec=grid_spec,
        out_shape=jax.ShapeDtypeStruct((ns, n_heads, LANES), jnp.int32),
        compiler_params=_cparams(("arbitrary", "arbitrary")),
        name="cache_select",
    )(page_table, *([cache_kt] * pp), q_s.reshape(ns, 1, w))


def _sample_attn_kernel(pt_ref, sel_ref, q_ref, kn_ref, vn_ref, ck_ref, cv_ref, o_ref,
                        kbuf, vbuf, sem, *, layer, n_heads):
    b = pl.program_id(0)
    ns = pl.num_programs(0)
    per_blk = MOBA_BLOCK // PAGE_SIZE
    slot = b % 2

    def copies(seq, sl):
        out = []
        for h in range(n_heads):
            rows = pl.ds(h * HEAD_DIM, HEAD_DIM)
            for r in range(MOBA_TOPK):
                blk = sel_ref[(seq * n_heads + h) * MOBA_TOPK + r]
                for j in range(per_blk):
                    page = pt_ref[seq, blk * per_blk + j]
                    cols = pl.ds((r * per_blk + j) * PAGE_SIZE, PAGE_SIZE)
                    out.append(pltpu.make_async_copy(ck_ref.at[layer, page, h],
                                                     kbuf.at[sl, rows, cols], sem.at[0, sl]))
                    out.append(pltpu.make_async_copy(cv_ref.at[layer, page, h],
                                                     vbuf.at[sl, rows, cols], sem.at[1, sl]))
        return out

    @pl.when(b == 0)
    def _():
        for c in copies(0, 0):
            c.start()

    @pl.when(b + 1 < ns)
    def _():
        for c in copies(b + 1, 1 - slot):
            c.start()

    for c in copies(b, slot):
        c.wait()

    w = n_heads * HEAD_DIM
    kb = kbuf[slot].astype(BF16)
    vb = vbuf[slot].astype(BF16)
    q = q_ref[0] * (HEAD_DIM ** -0.5)
    hrow = lax.broadcasted_iota(jnp.int32, (n_heads, w), 0)
    hlane = lax.broadcasted_iota(jnp.int32, (n_heads, w), 1)
    mine = (hlane >> 6) == hrow
    qbd = jnp.where(mine, jnp.broadcast_to(q, (n_heads, w)), 0.0)
    s = _dot(qbd.astype(BF16), kb)
    s_new = jnp.sum(qbd * kn_ref[0], axis=-1, keepdims=True)
    m = jnp.maximum(jnp.max(s, axis=-1, keepdims=True), s_new)
    p = jnp.exp(s - m)
    p_new = jnp.exp(s_new - m)
    den = jnp.sum(p, axis=-1, keepdims=True) + p_new
    o_all = _dot_nt(p.astype(BF16), vb) + p_new * vn_ref[0]
    o = jnp.sum(jnp.where(mine, o_all / den, 0.0), axis=0, keepdims=True)
    o_ref[0] = o.astype(o_ref.dtype)


def _sample_attention(q_s, k_s, v_s, cache_kt, cache_vt, page_table, sel_flat, l):
    ns, w = q_s.shape
    n_heads = w // HEAD_DIM
    cols = MOBA_TOPK * MOBA_BLOCK
    q3, k3, v3 = (a.reshape(ns, 1, w) for a in (q_s, k_s, v_s))
    row = lambda b, pt, sel: (b, 0, 0)
    grid_spec = pltpu.PrefetchScalarGridSpec(
        num_scalar_prefetch=2,
        grid=(ns,),
        in_specs=[pl.BlockSpec((1, 1, w), row), pl.BlockSpec((1, 1, w), row), pl.BlockSpec((1, 1, w), row),
                  pl.BlockSpec(memory_space=pl.ANY), pl.BlockSpec(memory_space=pl.ANY)],
        out_specs=pl.BlockSpec((1, 1, w), row),
        scratch_shapes=[pltpu.VMEM((2, w, cols), F32),
                        pltpu.VMEM((2, w, cols), F32),
                        pltpu.SemaphoreType.DMA((2, 2))],
    )
    out = pl.pallas_call(
        functools.partial(_sample_attn_kernel, layer=l, n_heads=n_heads),
        grid_spec=grid_spec,
        out_shape=jax.ShapeDtypeStruct((ns, 1, w), BF16),
        compiler_params=_cparams(("arbitrary",)),
        name="sample_attention",
    )(page_table, sel_flat, q3, k3, v3, cache_kt, cache_vt)
    return out.reshape(ns, w)


def _rope_tables(pos, n_heads):
    half = HEAD_DIM // 2
    inv_freq = ROPE_THETA ** (-jnp.arange(half, dtype=F32) / half)
    ang = pos.astype(F32)[:, None] * inv_freq[None, :]
    cos, sin = jnp.cos(ang), jnp.sin(ang)
    cos_t = jnp.tile(jnp.concatenate([cos, cos], axis=-1), (1, n_heads))
    sin_t = jnp.tile(jnp.concatenate([-sin, sin], axis=-1), (1, n_heads))
    return cos_t, sin_t


def _ssm_tables(a_re, a_im, log_dt, b_re, b_im, c_re, c_im):
    depth, groups, p = a_re.shape
    lam_re = jnp.minimum(a_re.astype(F32), DT_EPS_LAM)
    lam_im = a_im.astype(F32)
    dt = jnp.exp(log_dt.astype(F32))[..., None]
    mag = jnp.exp(lam_re * dt)
    ab_re = mag * jnp.cos(lam_im * dt)
    ab_im = mag * jnp.sin(lam_im * dt)
    den = lam_re * lam_re + lam_im * lam_im
    num_re = ab_re - 1.0
    f_re = (num_re * lam_re + ab_im * lam_im) / den
    f_im = (ab_im * lam_re - num_re * lam_im) / den
    br, bi = b_re.astype(F32), b_im.astype(F32)
    bb_re = f_re[..., None] * br - f_im[..., None] * bi
    bb_im = f_re[..., None] * bi + f_im[..., None] * br
    eye2 = jnp.eye(2, dtype=F32)

    def b_stack(bb):
        t = bb.transpose(0, 1, 3, 2).reshape(depth, groups // 2, 2, SSM_GROUP, p)
        t = t[:, :, :, :, None, :] * eye2[None, None, :, None, :, None]
        return t.reshape(depth, groups * SSM_GROUP, 2 * p).astype(BF16)

    def c_stack(c):
        t = c.astype(F32).reshape(depth, groups // 2, 2, SSM_GROUP, p)
        t = t[:, :, :, :, None, :] * eye2[None, None, :, None, :, None]
        t = t.transpose(0, 4, 5, 1, 2, 3)
        return t.reshape(depth, 2 * p, groups * SSM_GROUP).astype(BF16)

    lay = lambda a: a.reshape(depth, SUBLANES, LANES)
    return (lay(ab_re), lay(ab_im), b_stack(bb_re), b_stack(bb_im), c_stack(c_re), c_stack(c_im))


def _repeat_matrix(n):
    return jnp.repeat(jnp.eye(n, dtype=BF16), SUBLANES, axis=0)


def kernel(x_prompt, x_sample, cache_k, cache_v, state_ssm_re, state_ssm_im, page_table, mix_pre_g, w_in, ssm_a_re, ssm_a_im, ssm_log_dt, ssm_b_re, ssm_b_im, ssm_c_re, ssm_c_im, ssm_d, ssm_w_glu, ssm_b_glu, sgu_ln_g, sgu_ln_b, sgu_w, sgu_b, w_out_a, w_out_b, w_out_c, w_o, mix_post_g, ffn_pre_g, w_up, w_down, ffn_post_g):
    bp, tp, d = x_prompt.shape
    bs, ts, _ = x_sample.shape
    depth = w_in.shape[0]
    n_heads = cache_k.shape[3]
    n_pages = page_table.shape[1]
    past_len = n_pages * PAGE_SIZE
    groups = ssm_a_re.shape[1]
    assert ts == 1 and tp % MOBA_BLOCK == 0 and past_len % MOBA_BLOCK == 0
    assert n_pages % min(PAGES_PER_STEP, n_pages) == 0 and groups * SSM_STATE == SUBLANES * LANES

    row3 = lambda a: a.reshape(depth, 1, a.shape[-1]).astype(F32)
    w_in_bf = w_in.astype(BF16)
    wa_bf, wb_bf, wc_bf, wo_bf = (w.astype(BF16) for w in (w_out_a, w_out_b, w_out_c, w_o))
    wup_bf, wdn_bf = w_up.astype(BF16), w_down.astype(BF16)
    wglu_bf = ssm_w_glu.astype(BF16)
    gpre, gpost, fpre, fpost = row3(mix_pre_g), row3(mix_post_g), row3(ffn_pre_g), row3(ffn_post_g)
    lng, lnb, d_skip, bglu = row3(sgu_ln_g), row3(sgu_ln_b), row3(ssm_d), row3(ssm_b_glu)
    cos_p, sin_p = _rope_tables(jnp.arange(tp, dtype=jnp.int32), n_heads)
    cos_s, sin_s = _rope_tables(jnp.full((bs,), past_len, dtype=jnp.int32), n_heads)
    ar, ai, bre, bim, cre, cim = _ssm_tables(ssm_a_re, ssm_a_im, ssm_log_dt, ssm_b_re, ssm_b_im,
                                             ssm_c_re, ssm_c_im)
    sgu_dim = sgu_ln_g.shape[-1] // SGU_GROUPS
    bs2d = jnp.repeat(sgu_b.astype(F32).transpose(0, 2, 1), sgu_dim, axis=-1)
    w00 = jnp.repeat(sgu_w[:, :, 0, 0].astype(F32), sgu_dim, axis=-1).reshape(depth, 1, -1)
    b0 = bs2d[:, 0:1, :]
    cls = (jnp.arange(groups * SSM_GROUP) // (2 * SSM_GROUP))[None, :] == jnp.arange(SUBLANES)[:, None]
    bre8, bim8 = (jnp.where(cls[None, :, :, None], m[:, None], 0).astype(BF16) for m in (bre, bim))
    cre8, cim8 = (jnp.where(cls[None, :, None, :], m[:, None], 0).astype(BF16) for m in (cre, cim))
    rep_s = _repeat_matrix(bs)
    ar_s, ai_s = jnp.tile(ar, (1, bs, 1)), jnp.tile(ai, (1, bs, 1))
    h0r = state_ssm_re.astype(F32).reshape(depth, bs * SUBLANES, LANES)
    h0i = state_ssm_im.astype(F32).reshape(depth, bs * SUBLANES, LANES)
    aw = n_heads * HEAD_DIM
    k0 = 256 + 2 * 256 + aw
    wkt_bf = w_in_bf[:, :, k0:k0 + aw].transpose(0, 2, 1)
    wvt_bf = w_in_bf[:, :, k0 + aw:k0 + 2 * aw].transpose(0, 2, 1)
    cos_pt, sin_pt = cos_p.T, sin_p.T
    cache_kt = cache_k.transpose(0, 1, 3, 4, 2)
    cache_vt = cache_v.transpose(0, 1, 3, 4, 2)

    xp = x_prompt.reshape(bp * tp, d)
    xs = x_sample.reshape(bs * ts, d)
    kv_stacks = (jnp.zeros((depth, bp, aw, tp), F32), jnp.zeros((depth, bp, aw, tp), F32))
    hrp_l, hip_l = [], []
    ks_l, vs_l, hrs_l, his_l, sv_l = [], [], [], [], []
    for l in range(depth):
        u, yb, q, kt_all, vt_all, kbf, v_t, gs, km = _inproj(
            xp, l, depth, w_in_bf, wkt_bf, wvt_bf, gpre, cos_p, sin_p, cos_pt, sin_pt, lng, lnb,
            sgu_w, bs2d, tp, kv_stacks)
        kv_stacks = (kt_all, vt_all)
        ya, hre, him = _ssm_scan(u.reshape(bp, tp, -1), l, SSM_TILE, bre8, bim8, cre8, cim8, ar, ai,
                                 d_skip, wglu_bf, bglu)
        yc = _attention(q.reshape(bp, tp, -1), kbf.reshape(bp, tp, -1), v_t,
                        km.reshape(bp, tp // MOBA_BLOCK, -1))
        xp = _merge(xp, ya.reshape(bp * tp, -1), yb, yc.reshape(bp * tp, -1), gs, l,
                    wa_bf, wb_bf, wc_bf, wo_bf, gpost, 512)
        xp = _ffn(xp, l, fpre, wup_bf, wdn_bf, fpost, 512)
        hrp_l.append(hre.reshape(bp, groups, SSM_STATE))
        hip_l.append(him.reshape(bp, groups, SSM_STATE))

        (ya_s, yb_s, q_s, k_s, v_s, gs_s, hr_s, hi_s, vrow) = _sample_pre(
            xs, l, w_in_bf, gpre, cos_s, sin_s, lng, lnb, w00, b0, rep_s, bre, bim, cre, cim,
            ar_s, ai_s, h0r, h0i, d_skip, wglu_bf, bglu)
        sel_s = _cache_select(cache_kt, page_table, q_s, l)
        sel_flat = sel_s[:, :, :MOBA_TOPK].reshape(-1)
        yc_s = _sample_attention(q_s, k_s, v_s, cache_kt, cache_vt, page_table, sel_flat, l)
        xs = _merge(xs, ya_s, yb_s, yc_s, gs_s, l, wa_bf, wb_bf, wc_bf, wo_bf, gpost, bs)
        xs = _ffn(xs, l, fpre, wup_bf, wdn_bf, fpost, bs)
        ks_l.append(k_s.reshape(bs, ts, n_heads, HEAD_DIM))
        vs_l.append(v_s.reshape(bs, ts, n_heads, HEAD_DIM))
        hrs_l.append(hr_s.reshape(bs, groups, SSM_STATE))
        his_l.append(hi_s.reshape(bs, groups, SSM_STATE))
        sv_l.append(vrow.reshape(bs, ts, -1))

    def rows_minor(stack):
        return stack.reshape(depth, bp, n_heads, HEAD_DIM, tp).transpose(0, 1, 4, 2, 3)

    return (xp.reshape(bp, tp, d), xs.reshape(bs, ts, d),
            rows_minor(kv_stacks[0]), rows_minor(kv_stacks[1]), jnp.stack(hrp_l), jnp.stack(hip_l),
            jnp.stack(ks_l), jnp.stack(vs_l), jnp.stack(hrs_l), jnp.stack(his_l), jnp.stack(sv_l))
```

```python
import functools
import math

import jax
import jax.numpy as jnp
from jax import lax
from jax.experimental import pallas as pl
from jax.experimental.pallas import tpu as pltpu

F32 = jnp.float32
BF16 = jnp.bfloat16

SSM_GROUP = 16
SSM_STATE = 64
CHUNK = 128
SGU_GROUPS = 4
HEAD_DIM = 64
MOBA_BLOCK = 256
MOBA_TOPK = 3
PAGE_SIZE = 128
ROPE_THETA = 10000.0
DT_EPS_LAM = -1e-4
EPS = 1e-6
NEG = -1e30

LANES = 128
SUBLANES = 8
VMEM_LIMIT = 56 * 1024 * 1024

PAGES_PER_STEP = 32
ATT_TILE = 512
SSM_TILE = 256


def _cparams(sem):
    return pltpu.CompilerParams(dimension_semantics=sem, vmem_limit_bytes=VMEM_LIMIT)


def _resident(block_shape, index_map):
    return pl.BlockSpec(block_shape, index_map, pipeline_mode=pl.Buffered(1))


def _rmsnorm(x, g):
    return x * lax.rsqrt(jnp.mean(x * x, axis=-1, keepdims=True) + EPS) * g


def _layernorm(v, g, b):
    mu = jnp.mean(v, axis=-1, keepdims=True)
    d = v - mu
    var = jnp.mean(d * d, axis=-1, keepdims=True)
    return d * lax.rsqrt(var + EPS) * g + b


def _gelu_tanh(x):
    c = math.sqrt(2.0 / math.pi)
    return 0.5 * x * (1.0 + jnp.tanh(c * (x + 0.044715 * (x * x * x))))


def _rope(x, cos, sin_signed):
    n = x.shape[-1]
    half = HEAD_DIM // 2
    lane = lax.broadcasted_iota(jnp.int32, x.shape, 1)
    first = (lane & (HEAD_DIM - 1)) < half
    swapped = jnp.where(first, pltpu.roll(x, n - half, axis=1), pltpu.roll(x, half, axis=1))
    return x * cos + swapped * sin_signed


def _dot(a, b):
    return jnp.dot(a, b, preferred_element_type=F32)


def _dot_nt(a, b, precision=None):
    return lax.dot_general(a, b, (((1,), (1,)), ((), ())), preferred_element_type=F32,
                           precision=precision)


def _ssm_expand(u, rep_ref, bre_ref, bim_ref):
    um = _dot(rep_ref[...], u.astype(BF16))
    row = lax.broadcasted_iota(jnp.int32, um.shape, 0)
    col = lax.broadcasted_iota(jnp.int32, um.shape, 1)
    keep = (row & (SUBLANES - 1)) == (col >> 5)
    um = jnp.where(keep, um, 0.0).astype(BF16)
    return _dot(um, bre_ref[...]), _dot(um, bim_ref[...])


def _ssm_readout(h_re, h_im, cre_ref, cim_ref):
    y2 = _dot(h_re.astype(BF16), cre_ref[...]) - _dot(h_im.astype(BF16), cim_ref[...])
    row = lax.broadcasted_iota(jnp.int32, y2.shape, 0)
    col = lax.broadcasted_iota(jnp.int32, y2.shape, 1)
    keep = (row & (SUBLANES - 1)) == (col >> 5)
    y2 = jnp.where(keep, y2, 0.0)
    n = y2.shape[0] // SUBLANES
    return jnp.sum(y2.reshape(n, SUBLANES, y2.shape[1]), axis=1)


def _ssm_glu(y, u, d_ref, wglu_ref, bglu_ref):
    y = y + d_ref[...] * u
    g = _gelu_tanh(y)
    return g * jax.nn.sigmoid(_dot(g.astype(BF16), wglu_ref[...]) + bglu_ref[...])


def _rope_rows(x, cos_t, sin_t):
    half = HEAD_DIM // 2
    parts = []
    for h0 in range(0, x.shape[0], HEAD_DIM):
        parts += [x[h0 + half:h0 + HEAD_DIM], x[h0:h0 + half]]
    return x * cos_t + jnp.concatenate(parts, axis=0) * sin_t


def _inproj_kernel(x_ref, gpre_ref, w_ref, wkt_ref, wvt_ref, cos_ref, sin_ref, cost_ref, sint_ref,
                   lng_ref, lnb_ref, ws_ref, bs_ref, *rest):
    u_ref, yb_ref, q_ref, kt_ref, v_ref, kbf_ref, vt_ref, gs_ref, km_ref = rest[-9:]
    tm = x_ref.shape[0]
    h = _rmsnorm(x_ref[...], gpre_ref[...]).astype(BF16)

    def seg(a, b):
        return _dot(h, w_ref[:, a:b])

    u_ref[...] = seg(0, 256)

    u_sgu = seg(256, 512)
    vn = _layernorm(seg(512, 768), lng_ref[...], lnb_ref[...])
    r = lax.broadcasted_iota(jnp.int32, (CHUNK, CHUNK), 0)
    c = lax.broadcasted_iota(jnp.int32, (CHUNK, CHUNK), 1)
    tril = c <= r
    grp = lax.broadcasted_iota(jnp.int32, (CHUNK, 256), 1) >> 6
    wmask = [jnp.where(tril, ws_ref[g], 0.0).astype(BF16) for g in range(SGU_GROUPS)]
    for ci in range(tm // CHUNK):
        rows = slice(ci * CHUNK, (ci + 1) * CHUNK)
        vc = vn[rows]
        mixed = bs_ref[...]
        for g in range(SGU_GROUPS):
            mixed = mixed + _dot(wmask[g], jnp.where(grp == g, vc, 0.0).astype(BF16))
        yb_ref[rows, :] = (u_sgu[rows] * mixed).astype(BF16)

    cos = cos_ref[...]
    sin = sin_ref[...]
    q_ref[...] = _rope(seg(768, 1280), cos, sin)
    k = _rope(seg(1280, 1792), cos, sin)
    kbf_ref[...] = k.astype(BF16)
    km_ref[0] = jnp.mean(k, axis=0, keepdims=True)
    kt_ref[...] = _rope_rows(_dot_nt(wkt_ref[...], h), cost_ref[...], sint_ref[...])
    v_t = _dot_nt(wvt_ref[...], h)
    v_ref[...] = v_t
    vt_ref[...] = v_t.astype(BF16)
    for j in range(3):
        gs_ref[:, j * 1024:(j + 1) * 1024] = jax.nn.sigmoid(
            seg(2304 + j * 1024, 2304 + (j + 1) * 1024)).astype(BF16)


def _inproj(x2d, l, depth, w_in_bf, wkt_bf, wvt_bf, gpre, cos_t, sin_t, cos_tt, sin_tt, lng, lnb, sgu_w, bs2d,
            seq_len, kv_stacks):
    m, d = x2d.shape
    tm = MOBA_BLOCK
    n_t = seq_len // tm
    n_seq = m // seq_len
    in_w = w_in_bf.shape[-1]
    row = lambda i: (i, 0)
    lyr3 = lambda i: (l, 0, 0)
    stack = jax.ShapeDtypeStruct((depth, n_seq, 512, seq_len), F32)
    stack_spec = pl.BlockSpec((None, None, 512, tm), lambda i: (l, i // n_t, 0, i % n_t))
    outs = (
        jax.ShapeDtypeStruct((m, 256), F32),
        jax.ShapeDtypeStruct((m, 256), BF16),
        jax.ShapeDtypeStruct((m, 512), F32),
        stack,
        stack,
        jax.ShapeDtypeStruct((m, 512), BF16),
        jax.ShapeDtypeStruct((512, m), BF16),
        jax.ShapeDtypeStruct((m, 3072), BF16),
        jax.ShapeDtypeStruct((m // tm, 1, 512), F32),
    )
    in_specs = [
        pl.BlockSpec((tm, d), row),
        pl.BlockSpec((None, 1, d), lyr3),
        _resident((None, d, in_w), lyr3),
        _resident((None, 512, d), lyr3),
        _resident((None, 512, d), lyr3),
        pl.BlockSpec((tm, 512), lambda i: (i % n_t, 0)),
        pl.BlockSpec((tm, 512), lambda i: (i % n_t, 0)),
        pl.BlockSpec((512, tm), lambda i: (0, i % n_t)),
        pl.BlockSpec((512, tm), lambda i: (0, i % n_t)),
        pl.BlockSpec((None, 1, 256), lyr3),
        pl.BlockSpec((None, 1, 256), lyr3),
        pl.BlockSpec((None, SGU_GROUPS, CHUNK, CHUNK), lambda i: (l, 0, 0, 0)),
        pl.BlockSpec((None, CHUNK, 256), lyr3),
    ]
    args = [x2d, gpre, w_in_bf, wkt_bf, wvt_bf, cos_t, sin_t, cos_tt, sin_tt, lng, lnb, sgu_w, bs2d]
    aliases = {len(args): 3, len(args) + 1: 4}
    in_specs += [pl.BlockSpec(memory_space=pl.ANY)] * 2
    args += list(kv_stacks)
    return pl.pallas_call(
        _inproj_kernel,
        grid=(m // tm,),
        in_specs=in_specs,
        out_specs=[
            pl.BlockSpec((tm, 256), row), pl.BlockSpec((tm, 256), row),
            pl.BlockSpec((tm, 512), row), stack_spec, stack_spec,
            pl.BlockSpec((tm, 512), row), pl.BlockSpec((512, tm), lambda i: (0, i)),
            pl.BlockSpec((tm, 3072), row),
            pl.BlockSpec((1, 1, 512), lambda i: (i, 0, 0)),
        ],
        out_shape=outs,
        input_output_aliases=aliases,
        compiler_params=_cparams(("arbitrary",)),
        name="inproj",
    )(*args)


def _ssm_kernel(u_ref, bcat_ref, ccat_ref, ar_ref, ai_ref,
                d_ref, wglu_ref, bglu_ref, ya_ref, hre_ref, him_ref, sre, sim):
    nb, tt = u_ref.shape[0], u_ref.shape[1]
    step = pl.program_id(0)

    @pl.when(step == 0)
    def _():
        hre_ref[...] = jnp.zeros_like(hre_ref)
        him_ref[...] = jnp.zeros_like(him_ref)

    for b in range(nb):
        ub = u_ref[b].astype(BF16)
        for j in range(SUBLANES):
            rows_j = pl.ds(j, tt, stride=SUBLANES)
            bu = _dot(ub, bcat_ref[j])
            sre[b, rows_j, :] = bu[:, :LANES]
            sim[b, rows_j, :] = bu[:, LANES:]

    ar = ar_ref[...]
    ai = ai_ref[...]

    def body(t, carry):
        off = pl.multiple_of(t * SUBLANES, SUBLANES)
        new = []
        for b in range(nb):
            hr, hi = carry[2 * b], carry[2 * b + 1]
            nr = ar * hr - ai * hi + sre[b, pl.ds(off, SUBLANES), :]
            ni = ar * hi + ai * hr + sim[b, pl.ds(off, SUBLANES), :]
            sre[b, pl.ds(off, SUBLANES), :] = nr
            sim[b, pl.ds(off, SUBLANES), :] = ni
            new += [nr, ni]
        return tuple(new)

    init = []
    for b in range(nb):
        init += [hre_ref[b], him_ref[b]]
    fin = lax.fori_loop(0, tt, body, tuple(init), unroll=8)
    for b in range(nb):
        hre_ref[b] = fin[2 * b]
        him_ref[b] = fin[2 * b + 1]

    for b in range(nb):
        y = jnp.zeros((tt, u_ref.shape[2]), F32)
        for j in range(SUBLANES):
            rows_j = pl.ds(j, tt, stride=SUBLANES)
            h_j = jnp.concatenate([sre[b, rows_j, :], sim[b, rows_j, :]], axis=1).astype(BF16)
            y = y + _dot(h_j, ccat_ref[j])
        ya_ref[b] = _ssm_glu(y, u_ref[b], d_ref, wglu_ref, bglu_ref).astype(BF16)


def _ssm_scan(u3, l, tt, bcat8, ccat8, ar, ai, d_skip, wglu_bf, bglu):
    nb, t, w = u3.shape
    lyr3 = lambda i: (l, 0, 0)
    lyr4 = lambda i: (l, 0, 0, 0)
    return pl.pallas_call(
        _ssm_kernel,
        grid=(t // tt,),
        in_specs=[
            pl.BlockSpec((nb, tt, w), lambda i: (0, i, 0)),
            pl.BlockSpec((None, SUBLANES, 256, 256), lyr4), pl.BlockSpec((None, SUBLANES, 256, 256), lyr4),
            pl.BlockSpec((None, SUBLANES, LANES), lyr3), pl.BlockSpec((None, SUBLANES, LANES), lyr3),
            pl.BlockSpec((None, 1, 256), lyr3),
            pl.BlockSpec((None, 256, 256), lyr3),
            pl.BlockSpec((None, 1, 256), lyr3),
        ],
        out_specs=[
            pl.BlockSpec((nb, tt, w), lambda i: (0, i, 0)),
            pl.BlockSpec((nb, SUBLANES, LANES), lambda i: (0, 0, 0)),
            pl.BlockSpec((nb, SUBLANES, LANES), lambda i: (0, 0, 0)),
        ],
        out_shape=(
            jax.ShapeDtypeStruct((nb, t, w), BF16),
            jax.ShapeDtypeStruct((nb, SUBLANES, LANES), F32),
            jax.ShapeDtypeStruct((nb, SUBLANES, LANES), F32),
        ),
        scratch_shapes=[pltpu.VMEM((nb, SUBLANES * tt, LANES), F32),
                        pltpu.VMEM((nb, SUBLANES * tt, LANES), F32)],
        compiler_params=_cparams(("arbitrary",)),
        name="ssm_scan",
    )(u3, bcat8, ccat8, ar, ai, d_skip, wglu_bf, bglu)


def _top3_rows(gate, blk, own):
    big = jnp.int32(1 << 20)
    g = jnp.where(blk < own, gate, NEG)
    sel = jnp.zeros(gate.shape, jnp.bool_)
    for _ in range(MOBA_TOPK):
        m = jnp.max(g, axis=0, keepdims=True)
        first = jnp.min(jnp.where(g == m, blk, big), axis=0, keepdims=True)
        pick = blk == first
        sel = sel | (pick & (blk < own))
        g = jnp.where(pick, -jnp.inf, g)
    return sel


def _split_bf16(x):
    hi = x.astype(BF16)
    return hi, (x - hi.astype(F32)).astype(BF16)


def _attn_kernel(q_ref, k_ref, vt_ref, km_ref, o_ref, kaug, qaug, acc, sbuf0, sbuf1):
    qt = pl.program_id(2)
    tq = q_ref.shape[1]
    tk = tq
    t = k_ref.shape[1]
    nblk = km_ref.shape[1]
    shift = MOBA_BLOCK.bit_length() - 1

    @pl.when(qt == 0)
    def _():
        def fill(i, carry):
            off = pl.multiple_of(i * tk, tk)
            kk = k_ref[0, pl.ds(off, tk), :].astype(F32)
            lane = lax.broadcasted_iota(jnp.int32, kk.shape, 1)
            blk = (lax.broadcasted_iota(jnp.int32, kk.shape, 0) + off) >> shift
            hot_a = jnp.where(lane - HEAD_DIM == blk, 1.0, 0.0)
            hot_b = jnp.where(lane == blk, 1.0, 0.0)
            kaug[0, pl.ds(off, tk), :] = jnp.where(lane < HEAD_DIM, kk, hot_a).astype(BF16)
            kaug[1, pl.ds(off, tk), :] = jnp.where(lane >= HEAD_DIM, kk, hot_b).astype(BF16)
            return carry
        lax.fori_loop(0, t // tk, fill, 0)

    q_t = q_ref[0].T
    km = km_ref[0]
    km_lane = lax.broadcasted_iota(jnp.int32, km.shape, 1)
    blk = lax.broadcasted_iota(jnp.int32, (nblk, tq), 0)
    col = lax.broadcasted_iota(jnp.int32, (nblk, tq), 1)
    own = (qt * tq + col) >> shift
    q_hi, q_lo = _split_bf16(q_t)
    qs = q_t * (HEAD_DIM ** -0.5 * math.log2(math.e))
    pad = jnp.zeros((HEAD_DIM - nblk, tq), F32)
    for e in range(2):
        k_hi, k_lo = _split_bf16(jnp.where((km_lane >> 6) == e, km, 0.0))
        gate = _dot(k_hi, q_hi) + _dot(k_lo, q_hi) + _dot(k_hi, q_lo)
        sel = _top3_rows(gate, blk, own)
        bias = jnp.where(sel | (blk == own), 0.0, NEG)
        if e == 0:
            qa = jnp.concatenate([qs[0:HEAD_DIM], bias, pad], axis=0)
        else:
            qa = jnp.concatenate([bias, pad, qs[HEAD_DIM:2 * HEAD_DIM]], axis=0)
        qaug[e] = qa.astype(BF16)

    acc[...] = jnp.zeros_like(acc)

    def scores(j, dst):
        off = pl.multiple_of(j * tk, tk)
        for e in range(2):
            dst[e] = _dot(kaug[e, pl.ds(off, tk), :], qaug[e])

    def step(j, carry, cur, nxt, last=False):
        if not last:
            scores(j + 1, nxt)
        off = pl.multiple_of(j * tk, tk)
        new = []
        for e in range(2):
            m, l = carry[2 * e], carry[2 * e + 1]
            s = cur[e]
            if last:
                r = lax.broadcasted_iota(jnp.int32, s.shape, 0)
                c = lax.broadcasted_iota(jnp.int32, s.shape, 1)
                s = jnp.where(r <= c, s, NEG)
            m_new = jnp.maximum(m, jnp.max(s, axis=0, keepdims=True))
            alpha = jnp.exp2(m - m_new)
            p = jnp.exp2(s - m_new)
            l_new = alpha * l + jnp.sum(p, axis=0, keepdims=True)
            v_e = vt_ref[e * HEAD_DIM:(e + 1) * HEAD_DIM, pl.ds(off, tk)]
            acc[e] = alpha * acc[e] + _dot(v_e, p.astype(BF16))
            new += [m_new, l_new]
        return tuple(new)

    m0 = jnp.full((1, tq), -1e38, F32)
    l0 = jnp.zeros((1, tq), F32)
    scores(0, sbuf0)

    def pair(i, carry):
        carry = step(2 * i, carry, sbuf0, sbuf1)
        return step(2 * i + 1, carry, sbuf1, sbuf0)

    carry = lax.fori_loop(0, qt // 2, pair, (m0, l0, m0, l0))

    def tail_even(carry):
        return step(qt, carry, sbuf0, sbuf1, last=True)

    def tail_odd(carry):
        carry = step(qt - 1, carry, sbuf0, sbuf1)
        return step(qt, carry, sbuf1, sbuf0, last=True)

    _, l_a, _, l_b = lax.cond(qt % 2 == 0, tail_even, tail_odd, carry)
    o_t = jnp.concatenate([acc[0] / l_a, acc[1] / l_b], axis=0)
    o_ref[0] = o_t.T.astype(BF16)


def _attention(q3, kbf3, v_t, km3):
    nb, t, w = q3.shape
    nblk = km3.shape[1]
    tq = min(ATT_TILE, t)
    assert nblk <= HEAD_DIM and t % tq == 0
    return pl.pallas_call(
        _attn_kernel,
        grid=(nb, w // LANES, t // tq),
        in_specs=[
            pl.BlockSpec((1, tq, LANES), lambda b, hp, i: (b, i, hp)),
            pl.BlockSpec((1, t, LANES), lambda b, hp, i: (b, 0, hp)),
            pl.BlockSpec((LANES, t), lambda b, hp, i: (hp, b)),
            pl.BlockSpec((1, nblk, LANES), lambda b, hp, i: (b, 0, hp)),
        ],
        out_specs=pl.BlockSpec((1, tq, LANES), lambda b, hp, i: (b, i, hp)),
        out_shape=jax.ShapeDtypeStruct((nb, t, w), BF16),
        scratch_shapes=[pltpu.VMEM((2, t, LANES), BF16),
                        pltpu.VMEM((2, LANES, tq), BF16),
                        pltpu.VMEM((2, HEAD_DIM, tq), F32),
                        pltpu.VMEM((2, tq, tq), F32),
                        pltpu.VMEM((2, tq, tq), F32)],
        compiler_params=_cparams(("arbitrary", "arbitrary", "arbitrary")),
        name="moba_attention",
    )(q3, kbf3, v_t, km3)


def _merge_kernel(x_ref, ya_ref, yb_ref, yc_ref, gs_ref, wa_ref, wb_ref, wc_ref, wo_ref, gpost_ref, o_ref):
    d = x_ref.shape[1]
    merged = (gs_ref[:, 0:d].astype(F32) * _dot(ya_ref[...], wa_ref[...])
              + gs_ref[:, d:2 * d].astype(F32) * _dot(yb_ref[...], wb_ref[...])
              + gs_ref[:, 2 * d:3 * d].astype(F32) * _dot(yc_ref[...], wc_ref[...]))
    mix = _dot(merged.astype(BF16), wo_ref[...])
    o_ref[...] = x_ref[...] + _rmsnorm(mix, gpost_ref[...])


def _merge(x2d, ya, yb, yc, gs, l, wa, wb, wc, wo, gpost, tm):
    m, d = x2d.shape
    row = lambda i: (i, 0)
    lyr = lambda i: (l, 0, 0)
    return pl.pallas_call(
        _merge_kernel,
        grid=(m // tm,),
        in_specs=[
            pl.BlockSpec((tm, d), row),
            pl.BlockSpec((tm, 256), row), pl.BlockSpec((tm, 256), row), pl.BlockSpec((tm, 512), row),
            pl.BlockSpec((tm, 3 * d), row),
            _resident((None, 256, d), lyr), _resident((None, 256, d), lyr),
            _resident((None, 512, d), lyr), _resident((None, d, d), lyr),
            pl.BlockSpec((None, 1, d), lyr),
        ],
        out_specs=pl.BlockSpec((tm, d), row),
        out_shape=jax.ShapeDtypeStruct((m, d), F32),
        compiler_params=_cparams(("arbitrary",)),
        name="merge",
    )(x2d, ya, yb, yc, gs, wa, wb, wc, wo, gpost)


def _ffn_kernel(x_ref, gpre_ref, wup_ref, wdn_ref, gpost_ref, o_ref):
    x = x_ref[...]
    h = _rmsnorm(x, gpre_ref[...]).astype(BF16)
    d = x.shape[1]
    dff = wup_ref.shape[1]
    acc = jnp.zeros(x.shape, F32)
    for j in range(dff // d):
        a = jnp.maximum(_dot(h, wup_ref[:, j * d:(j + 1) * d]), 0.0)
        acc = acc + _dot((a * a).astype(BF16), wdn_ref[j * d:(j + 1) * d, :])
    o_ref[...] = x + _rmsnorm(acc, gpost_ref[...])


def _ffn(x2d, l, gpre, wup, wdn, gpost, tm):
    m, d = x2d.shape
    dff = wup.shape[-1]
    row = lambda i: (i, 0)
    lyr = lambda i: (l, 0, 0)
    return pl.pallas_call(
        _ffn_kernel,
        grid=(m // tm,),
        in_specs=[
            pl.BlockSpec((tm, d), row),
            pl.BlockSpec((None, 1, d), lyr),
            _resident((None, d, dff), lyr), _resident((None, dff, d), lyr),
            pl.BlockSpec((None, 1, d), lyr),
        ],
        out_specs=pl.BlockSpec((tm, d), row),
        out_shape=jax.ShapeDtypeStruct((m, d), F32),
        compiler_params=_cparams(("arbitrary",)),
        name="ffn",
    )(x2d, gpre, wup, wdn, gpost)


def _sample_pre_kernel(x_ref, gpre_ref, w_ref, cos_ref, sin_ref, lng_ref, lnb_ref, w00_ref, b0_ref,
                       rep_ref, bre_ref, bim_ref, cre_ref, cim_ref, ar_ref, ai_ref, h0r_ref, h0i_ref,
                       d_ref, wglu_ref, bglu_ref,
                       ya_ref, yb_ref, q_ref, k_ref, v_ref, gs_ref, hr_ref, hi_ref, vrow_ref):
    h = _rmsnorm(x_ref[...], gpre_ref[...]).astype(BF16)

    def seg(a, b):
        return _dot(h, w_ref[:, a:b])

    u = seg(0, 256)
    bu_re, bu_im = _ssm_expand(u, rep_ref, bre_ref, bim_ref)
    ar, ai = ar_ref[...], ai_ref[...]
    h0r, h0i = h0r_ref[...], h0i_ref[...]
    hr = ar * h0r - ai * h0i + bu_re
    hi = ar * h0i + ai * h0r + bu_im
    hr_ref[...] = hr
    hi_ref[...] = hi
    y = _ssm_readout(hr, hi, cre_ref, cim_ref)
    ya_ref[...] = _ssm_glu(y, u, d_ref, wglu_ref, bglu_ref).astype(BF16)

    vn = _layernorm(seg(512, 768), lng_ref[...], lnb_ref[...])
    vrow_ref[...] = vn
    yb_ref[...] = (seg(256, 512) * (w00_ref[...] * vn + b0_ref[...])).astype(BF16)

    cos, sin = cos_ref[...], sin_ref[...]
    q_ref[...] = _rope(seg(768, 1280), cos, sin)
    k_ref[...] = _rope(seg(1280, 1792), cos, sin)
    v_ref[...] = seg(1792, 2304)
    for j in range(3):
        gs_ref[:, j * 1024:(j + 1) * 1024] = jax.nn.sigmoid(
            seg(2304 + j * 1024, 2304 + (j + 1) * 1024)).astype(BF16)


def _sample_pre(xs, l, w_in_bf, gpre, cos_s, sin_s, lng, lnb, w00, b0, rep_s, bre, bim, cre, cim,
                ar_s, ai_s, h0r, h0i, d_skip, wglu_bf, bglu):
    ns, d = xs.shape
    in_w = w_in_bf.shape[-1]
    c2 = lambda i: (0, 0)
    lyr = lambda i: (l, 0, 0)
    full = lambda a: pl.BlockSpec(a.shape, c2)
    outs = (
        jax.ShapeDtypeStruct((ns, 256), BF16), jax.ShapeDtypeStruct((ns, 256), BF16),
        jax.ShapeDtypeStruct((ns, 512), F32), jax.ShapeDtypeStruct((ns, 512), F32),
        jax.ShapeDtypeStruct((ns, 512), F32), jax.ShapeDtypeStruct((ns, 3072), BF16),
        jax.ShapeDtypeStruct((ns * SUBLANES, LANES), F32), jax.ShapeDtypeStruct((ns * SUBLANES, LANES), F32),
        jax.ShapeDtypeStruct((ns, 256), F32),
    )
    return pl.pallas_call(
        _sample_pre_kernel,
        grid=(1,),
        in_specs=[
            full(xs),
            pl.BlockSpec((None, 1, d), lyr),
            pl.BlockSpec((None, d, in_w), lyr),
            full(cos_s), full(sin_s),
            pl.BlockSpec((None, 1, 256), lyr), pl.BlockSpec((None, 1, 256), lyr),
            pl.BlockSpec((None, 1, 256), lyr), pl.BlockSpec((None, 1, 256), lyr),
            full(rep_s),
            pl.BlockSpec((None, 256, LANES), lyr), pl.BlockSpec((None, 256, LANES), lyr),
            pl.BlockSpec((None, LANES, 256), lyr), pl.BlockSpec((None, LANES, 256), lyr),
            pl.BlockSpec((None, ns * SUBLANES, LANES), lyr), pl.BlockSpec((None, ns * SUBLANES, LANES), lyr),
            pl.BlockSpec((None, ns * SUBLANES, LANES), lyr), pl.BlockSpec((None, ns * SUBLANES, LANES), lyr),
            pl.BlockSpec((None, 1, 256), lyr),
            pl.BlockSpec((None, 256, 256), lyr),
            pl.BlockSpec((None, 1, 256), lyr),
        ],
        out_specs=[pl.BlockSpec(o.shape, c2) for o in outs],
        out_shape=outs,
        compiler_params=_cparams(("arbitrary",)),
        name="sample_pre",
    )(xs, gpre, w_in_bf, cos_s, sin_s, lng, lnb, w00, b0, rep_s, bre, bim, cre, cim,
      ar_s, ai_s, h0r, h0i, d_skip, wglu_bf, bglu)


def _cache_select_kernel(pt_ref, q_ref, ck_ref, sel_ref, kmt, pbuf, sem, *, layer, nblk):
    b, j = pl.program_id(0), pl.program_id(1)
    n_j = pl.num_programs(1)
    step = b * n_j + j
    n_steps = pl.num_programs(0) * n_j
    slot = step % 2
    pp = pbuf.shape[1]
    per_blk = MOBA_BLOCK // PAGE_SIZE
    blk_per_step = pp // per_blk
    rows = kmt.shape[0]

    def copies(seq, chunk, sl):
        return [pltpu.make_async_copy(ck_ref.at[layer, pt_ref[seq, chunk * pp + i]], pbuf.at[sl, i], sem.at[sl])
                for i in range(pp)]

    @pl.when(step == 0)
    def _():
        for c in copies(0, 0, 0):
            c.start()

    @pl.when(step + 1 < n_steps)
    def _():
        wrap = j + 1 == n_j
        for c in copies(jnp.where(wrap, b + 1, b), jnp.where(wrap, 0, j + 1), 1 - slot):
            c.start()

    for c in copies(b, j, slot):
        c.wait()

    @pl.when(j == 0)
    def _():
        kmt[...] = jnp.zeros_like(kmt)

    col = lax.broadcasted_iota(jnp.int32, (PAGE_SIZE, LANES), 1)
    tot = jnp.zeros(kmt.shape, F32)
    for i in range(blk_per_step):
        x = pbuf[slot, per_blk * i].reshape(rows, PAGE_SIZE)
        for jj in range(1, per_blk):
            x = x + pbuf[slot, per_blk * i + jj].reshape(rows, PAGE_SIZE)
        pick = jnp.where(col == j * blk_per_step + i, 1.0, 0.0).astype(BF16)
        x_hi, x_lo = _split_bf16(x)
        tot = tot + _dot(x_hi, pick) + _dot(x_lo, pick)
    acc = kmt[...] + tot * (1.0 / MOBA_BLOCK)
    kmt[...] = acc

    @pl.when(j == pl.num_programs(1) - 1)
    def _():
        n_heads = sel_ref.shape[1]
        q = q_ref[0]
        hrow = lax.broadcasted_iota(jnp.int32, (n_heads, rows), 0)
        hlane = lax.broadcasted_iota(jnp.int32, (n_heads, rows), 1)
        qbd = jnp.where((hlane >> 6) == hrow, jnp.broadcast_to(q, (n_heads, rows)), 0.0)
        gate = jnp.dot(qbd, acc, preferred_element_type=F32, precision=lax.Precision.HIGHEST)
        idx = lax.broadcasted_iota(jnp.int32, gate.shape, 1)
        big = jnp.int32(1 << 20)
        g = jnp.where(idx < nblk, gate, -jnp.inf)
        out = jnp.zeros(gate.shape, jnp.int32)
        for r in range(MOBA_TOPK):
            m = jnp.max(g, axis=-1, keepdims=True)
            first = jnp.min(jnp.where(g == m, idx, big), axis=-1, keepdims=True)
            out = jnp.where(idx == r, first, out)
            g = jnp.where(idx == first, -jnp.inf, g)
        sel_ref[0] = out


def _cache_select(cache_kt, page_table, q_s, l):
    ns, n_pages = page_table.shape
    n_heads, hd, psz = cache_kt.shape[2:]
    w = n_heads * hd
    pp = min(PAGES_PER_STEP, n_pages)
    nblk = n_pages * psz // MOBA_BLOCK
    assert nblk <= LANES and nblk >= MOBA_TOPK

    grid_spec = pltpu.PrefetchScalarGridSpec(
        num_scalar_prefetch=1,
        grid=(ns, n_pages // pp),
        in_specs=[pl.BlockSpec((1, 1, w), lambda b, j, pt: (b, 0, 0)), pl.BlockSpec(memory_space=pl.ANY)],
        out_specs=pl.BlockSpec((1, n_heads, LANES), lambda b, j, pt: (b, 0, 0)),
        scratch_shapes=[pltpu.VMEM((w, LANES), F32),
                        pltpu.VMEM((2, pp, n_heads, hd, psz), F32),
                        pltpu.SemaphoreType.DMA((2,))],
    )
    return pl.pallas_call(
        functools.partial(_cache_select_kernel, layer=l, nblk=nblk),
        grid_spec=grid_spec,
        out_shape=jax.ShapeDtypeStruct((ns, n_heads, LANES), jnp.int32),
        compiler_params=_cparams(("arbitrary", "arbitrary")),
        name="cache_select",
    )(page_table, q_s.reshape(ns, 1, w), cache_kt)


def _sample_attn_kernel(pt_ref, sel_ref, q_ref, kn_ref, vn_ref, ck_ref, cv_ref, o_ref,
                        kbuf, vbuf, sem, *, layer, n_heads):
    b = pl.program_id(0)
    ns = pl.num_programs(0)
    per_blk = MOBA_BLOCK // PAGE_SIZE
    slot = b % 2

    def copies(seq, sl):
        out = []
        for h in range(n_heads):
            rows = pl.ds(h * HEAD_DIM, HEAD_DIM)
            for r in range(MOBA_TOPK):
                blk = sel_ref[(seq * n_heads + h) * MOBA_TOPK + r]
                for j in range(per_blk):
                    page = pt_ref[seq, blk * per_blk + j]
                    cols = pl.ds((r * per_blk + j) * PAGE_SIZE, PAGE_SIZE)
                    out.append(pltpu.make_async_copy(ck_ref.at[layer, page, h],
                                                     kbuf.at[sl, rows, cols], sem.at[0, sl]))
                    out.append(pltpu.make_async_copy(cv_ref.at[layer, page, h],
                                                     vbuf.at[sl, rows, cols], sem.at[1, sl]))
        return out

    @pl.when(b == 0)
    def _():
        for c in copies(0, 0):
            c.start()

    @pl.when(b + 1 < ns)
    def _():
        for c in copies(b + 1, 1 - slot):
            c.start()

    for c in copies(b, slot):
        c.wait()

    w = n_heads * HEAD_DIM
    kb = kbuf[slot].astype(BF16)
    vb = vbuf[slot].astype(BF16)
    q = q_ref[0] * (HEAD_DIM ** -0.5)
    hrow = lax.broadcasted_iota(jnp.int32, (n_heads, w), 0)
    hlane = lax.broadcasted_iota(jnp.int32, (n_heads, w), 1)
    mine = (hlane >> 6) == hrow
    qbd = jnp.where(mine, jnp.broadcast_to(q, (n_heads, w)), 0.0)
    s = _dot(qbd.astype(BF16), kb)
    s_new = jnp.sum(qbd * kn_ref[0], axis=-1, keepdims=True)
    m = jnp.maximum(jnp.max(s, axis=-1, keepdims=True), s_new)
    p = jnp.exp(s - m)
    p_new = jnp.exp(s_new - m)
    den = jnp.sum(p, axis=-1, keepdims=True) + p_new
    o_all = _dot_nt(p.astype(BF16), vb) + p_new * vn_ref[0]
    o = jnp.sum(jnp.where(mine, o_all / den, 0.0), axis=0, keepdims=True)
    o_ref[0] = o.astype(o_ref.dtype)


def _sample_attention(q_s, k_s, v_s, cache_kt, cache_vt, page_table, sel_flat, l):
    ns, w = q_s.shape
    n_heads = w // HEAD_DIM
    cols = MOBA_TOPK * MOBA_BLOCK
    q3, k3, v3 = (a.reshape(ns, 1, w) for a in (q_s, k_s, v_s))
    row = lambda b, pt, sel: (b, 0, 0)
    grid_spec = pltpu.PrefetchScalarGridSpec(
        num_scalar_prefetch=2,
        grid=(ns,),
        in_specs=[pl.BlockSpec((1, 1, w), row), pl.BlockSpec((1, 1, w), row), pl.BlockSpec((1, 1, w), row),
                  pl.BlockSpec(memory_space=pl.ANY), pl.BlockSpec(memory_space=pl.ANY)],
        out_specs=pl.BlockSpec((1, 1, w), row),
        scratch_shapes=[pltpu.VMEM((2, w, cols), F32),
                        pltpu.VMEM((2, w, cols), F32),
                        pltpu.SemaphoreType.DMA((2, 2))],
    )
    out = pl.pallas_call(
        functools.partial(_sample_attn_kernel, layer=l, n_heads=n_heads),
        grid_spec=grid_spec,
        out_shape=jax.ShapeDtypeStruct((ns, 1, w), BF16),
        compiler_params=_cparams(("arbitrary",)),
        name="sample_attention",
    )(page_table, sel_flat, q3, k3, v3, cache_kt, cache_vt)
    return out.reshape(ns, w)


def _rope_tables(pos, n_heads):
    half = HEAD_DIM // 2
    inv_freq = ROPE_THETA ** (-jnp.arange(half, dtype=F32) / half)
    ang = pos.astype(F32)[:, None] * inv_freq[None, :]
    cos, sin = jnp.cos(ang), jnp.sin(ang)
    cos_t = jnp.tile(jnp.concatenate([cos, cos], axis=-1), (1, n_heads))
    sin_t = jnp.tile(jnp.concatenate([-sin, sin], axis=-1), (1, n_heads))
    return cos_t, sin_t


def _ssm_tables(a_re, a_im, log_dt, b_re, b_im, c_re, c_im):
    depth, groups, p = a_re.shape
    lam_re = jnp.minimum(a_re.astype(F32), DT_EPS_LAM)
    lam_im = a_im.astype(F32)
    dt = jnp.exp(log_dt.astype(F32))[..., None]
    mag = jnp.exp(lam_re * dt)
    ab_re = mag * jnp.cos(lam_im * dt)
    ab_im = mag * jnp.sin(lam_im * dt)
    den = lam_re * lam_re + lam_im * lam_im
    num_re = ab_re - 1.0
    f_re = (num_re * lam_re + ab_im * lam_im) / den
    f_im = (ab_im * lam_re - num_re * lam_im) / den
    br, bi = b_re.astype(F32), b_im.astype(F32)
    bb_re = f_re[..., None] * br - f_im[..., None] * bi
    bb_im = f_re[..., None] * bi + f_im[..., None] * br
    eye2 = jnp.eye(2, dtype=F32)

    def b_stack(bb):
        t = bb.transpose(0, 1, 3, 2).reshape(depth, groups // 2, 2, SSM_GROUP, p)
        t = t[:, :, :, :, None, :] * eye2[None, None, :, None, :, None]
        return t.reshape(depth, groups * SSM_GROUP, 2 * p).astype(BF16)

    def c_stack(c):
        t = c.astype(F32).reshape(depth, groups // 2, 2, SSM_GROUP, p)
        t = t[:, :, :, :, None, :] * eye2[None, None, :, None, :, None]
        t = t.transpose(0, 4, 5, 1, 2, 3)
        return t.reshape(depth, 2 * p, groups * SSM_GROUP).astype(BF16)

    lay = lambda a: a.reshape(depth, SUBLANES, LANES)
    return (lay(ab_re), lay(ab_im), b_stack(bb_re), b_stack(bb_im), c_stack(c_re), c_stack(c_im))


def _repeat_matrix(n):
    return jnp.repeat(jnp.eye(n, dtype=BF16), SUBLANES, axis=0)


def kernel(x_prompt, x_sample, cache_k, cache_v, state_ssm_re, state_ssm_im, page_table, mix_pre_g, w_in, ssm_a_re, ssm_a_im, ssm_log_dt, ssm_b_re, ssm_b_im, ssm_c_re, ssm_c_im, ssm_d, ssm_w_glu, ssm_b_glu, sgu_ln_g, sgu_ln_b, sgu_w, sgu_b, w_out_a, w_out_b, w_out_c, w_o, mix_post_g, ffn_pre_g, w_up, w_down, ffn_post_g):
    bp, tp, d = x_prompt.shape
    bs, ts, _ = x_sample.shape
    depth = w_in.shape[0]
    n_heads = cache_k.shape[3]
    n_pages = page_table.shape[1]
    past_len = n_pages * PAGE_SIZE
    groups = ssm_a_re.shape[1]
    assert ts == 1 and tp % MOBA_BLOCK == 0 and past_len % MOBA_BLOCK == 0
    assert n_pages % min(PAGES_PER_STEP, n_pages) == 0 and groups * SSM_STATE == SUBLANES * LANES

    row3 = lambda a: a.reshape(depth, 1, a.shape[-1]).astype(F32)
    w_in_bf = w_in.astype(BF16)
    wa_bf, wb_bf, wc_bf, wo_bf = (w.astype(BF16) for w in (w_out_a, w_out_b, w_out_c, w_o))
    wup_bf, wdn_bf = w_up.astype(BF16), w_down.astype(BF16)
    wglu_bf = ssm_w_glu.astype(BF16)
    gpre, gpost, fpre, fpost = row3(mix_pre_g), row3(mix_post_g), row3(ffn_pre_g), row3(ffn_post_g)
    lng, lnb, d_skip, bglu = row3(sgu_ln_g), row3(sgu_ln_b), row3(ssm_d), row3(ssm_b_glu)
    cos_p, sin_p = _rope_tables(jnp.arange(tp, dtype=jnp.int32), n_heads)
    cos_s, sin_s = _rope_tables(jnp.full((bs,), past_len, dtype=jnp.int32), n_heads)
    ar, ai, bre, bim, cre, cim = _ssm_tables(ssm_a_re, ssm_a_im, ssm_log_dt, ssm_b_re, ssm_b_im,
                                             ssm_c_re, ssm_c_im)
    sgu_dim = sgu_ln_g.shape[-1] // SGU_GROUPS
    bs2d = jnp.repeat(sgu_b.astype(F32).transpose(0, 2, 1), sgu_dim, axis=-1)
    w00 = jnp.repeat(sgu_w[:, :, 0, 0].astype(F32), sgu_dim, axis=-1).reshape(depth, 1, -1)
    b0 = bs2d[:, 0:1, :]
    cls = (jnp.arange(groups * SSM_GROUP) // (2 * SSM_GROUP))[None, :] == jnp.arange(SUBLANES)[:, None]
    bcat8 = jnp.where(cls[None, :, :, None], jnp.concatenate([bre, bim], axis=-1)[:, None], 0).astype(BF16)
    ccat8 = jnp.where(cls[None, :, None, :], jnp.concatenate([cre, -cim], axis=-2)[:, None], 0).astype(BF16)
    rep_s = _repeat_matrix(bs)
    ar_s, ai_s = jnp.tile(ar, (1, bs, 1)), jnp.tile(ai, (1, bs, 1))
    h0r = state_ssm_re.astype(F32).reshape(depth, bs * SUBLANES, LANES)
    h0i = state_ssm_im.astype(F32).reshape(depth, bs * SUBLANES, LANES)
    aw = n_heads * HEAD_DIM
    k0 = 256 + 2 * 256 + aw
    wkt_bf = w_in_bf[:, :, k0:k0 + aw].transpose(0, 2, 1)
    wvt_bf = w_in_bf[:, :, k0 + aw:k0 + 2 * aw].transpose(0, 2, 1)
    cos_pt, sin_pt = cos_p.T, sin_p.T
    cache_kt = cache_k.transpose(0, 1, 3, 4, 2)
    cache_vt = cache_v.transpose(0, 1, 3, 4, 2)

    xp = x_prompt.reshape(bp * tp, d)
    xs = x_sample.reshape(bs * ts, d)
    kv_stacks = (jnp.zeros((depth, bp, aw, tp), F32), jnp.zeros((depth, bp, aw, tp), F32))
    hrp_l, hip_l = [], []
    ks_l, vs_l, hrs_l, his_l, sv_l = [], [], [], [], []
    for l in range(depth):
        u, yb, q, kt_all, vt_all, kbf, v_t, gs, km = _inproj(
            xp, l, depth, w_in_bf, wkt_bf, wvt_bf, gpre, cos_p, sin_p, cos_pt, sin_pt, lng, lnb,
            sgu_w, bs2d, tp, kv_stacks)
        kv_stacks = (kt_all, vt_all)
        ya, hre, him = _ssm_scan(u.reshape(bp, tp, -1), l, SSM_TILE, bcat8, ccat8, ar, ai,
                                 d_skip, wglu_bf, bglu)
        yc = _attention(q.reshape(bp, tp, -1), kbf.reshape(bp, tp, -1), v_t,
                        km.reshape(bp, tp // MOBA_BLOCK, -1))
        xp = _merge(xp, ya.reshape(bp * tp, -1), yb, yc.reshape(bp * tp, -1), gs, l,
                    wa_bf, wb_bf, wc_bf, wo_bf, gpost, 512)
        xp = _ffn(xp, l, fpre, wup_bf, wdn_bf, fpost, 512)
        hrp_l.append(hre.reshape(bp, groups, SSM_STATE))
        hip_l.append(him.reshape(bp, groups, SSM_STATE))

        (ya_s, yb_s, q_s, k_s, v_s, gs_s, hr_s, hi_s, vrow) = _sample_pre(
            xs, l, w_in_bf, gpre, cos_s, sin_s, lng, lnb, w00, b0, rep_s, bre, bim, cre, cim,
            ar_s, ai_s, h0r, h0i, d_skip, wglu_bf, bglu)
        sel_s = _cache_select(cache_kt, page_table, q_s, l)
        sel_flat = sel_s[:, :, :MOBA_TOPK].reshape(-1)
        yc_s = _sample_attention(q_s, k_s, v_s, cache_kt, cache_vt, page_table, sel_flat, l)
        xs = _merge(xs, ya_s, yb_s, yc_s, gs_s, l, wa_bf, wb_bf, wc_bf, wo_bf, gpost, bs)
        xs = _ffn(xs, l, fpre, wup_bf, wdn_bf, fpost, bs)
        ks_l.append(k_s.reshape(bs, ts, n_heads, HEAD_DIM))
        vs_l.append(v_s.reshape(bs, ts, n_heads, HEAD_DIM))
        hrs_l.append(hr_s.reshape(bs, groups, SSM_STATE))
        his_l.append(hi_s.reshape(bs, groups, SSM_STATE))
        sv_l.append(vrow.reshape(bs, ts, -1))

    def rows_minor(stack):
        return stack.reshape(depth, bp, n_heads, HEAD_DIM, tp).transpose(0, 1, 4, 2, 3)

    return (xp.reshape(bp, tp, d), xs.reshape(bs, ts, d),
            rows_minor(kv_stacks[0]), rows_minor(kv_stacks[1]), jnp.stack(hrp_l), jnp.stack(hip_l),
            jnp.stack(ks_l), jnp.stack(vs_l), jnp.stack(hrs_l), jnp.stack(his_l), jnp.stack(sv_l))
```

```python
import functools
import math

import jax
import jax.numpy as jnp
from jax import lax
from jax.experimental import pallas as pl
from jax.experimental.pallas import tpu as pltpu

F32 = jnp.float32
BF16 = jnp.bfloat16

SSM_GROUP = 16
SSM_STATE = 64
CHUNK = 128
SGU_GROUPS = 4
HEAD_DIM = 64
MOBA_BLOCK = 256
MOBA_TOPK = 3
PAGE_SIZE = 128
ROPE_THETA = 10000.0
DT_EPS_LAM = -1e-4
EPS = 1e-6
NEG = -1e30

LANES = 128
SUBLANES = 8
VMEM_LIMIT = 56 * 1024 * 1024

PAGES_PER_STEP = 32
ATT_TILE = 512
SSM_TILE = 256
VAUG_ROWS = HEAD_DIM + 16


def _cparams(sem):
    return pltpu.CompilerParams(dimension_semantics=sem, vmem_limit_bytes=VMEM_LIMIT)


def _resident(block_shape, index_map):
    return pl.BlockSpec(block_shape, index_map, pipeline_mode=pl.Buffered(1))


def _rmsnorm(x, g):
    return x * lax.rsqrt(jnp.mean(x * x, axis=-1, keepdims=True) + EPS) * g


def _layernorm(v, g, b):
    mu = jnp.mean(v, axis=-1, keepdims=True)
    d = v - mu
    var = jnp.mean(d * d, axis=-1, keepdims=True)
    return d * lax.rsqrt(var + EPS) * g + b


def _gelu_tanh(x):
    c = math.sqrt(2.0 / math.pi)
    return 0.5 * x * (1.0 + jnp.tanh(c * (x + 0.044715 * (x * x * x))))


def _rope(x, cos, sin_signed):
    n = x.shape[-1]
    half = HEAD_DIM // 2
    lane = lax.broadcasted_iota(jnp.int32, x.shape, 1)
    first = (lane & (HEAD_DIM - 1)) < half
    swapped = jnp.where(first, pltpu.roll(x, n - half, axis=1), pltpu.roll(x, half, axis=1))
    return x * cos + swapped * sin_signed


def _dot(a, b):
    return jnp.dot(a, b, preferred_element_type=F32)


def _dot_nt(a, b, precision=None):
    return lax.dot_general(a, b, (((1,), (1,)), ((), ())), preferred_element_type=F32,
                           precision=precision)


def _ssm_expand(u, rep_ref, bre_ref, bim_ref):
    um = _dot(rep_ref[...], u.astype(BF16))
    row = lax.broadcasted_iota(jnp.int32, um.shape, 0)
    col = lax.broadcasted_iota(jnp.int32, um.shape, 1)
    keep = (row & (SUBLANES - 1)) == (col >> 5)
    um = jnp.where(keep, um, 0.0).astype(BF16)
    return _dot(um, bre_ref[...]), _dot(um, bim_ref[...])


def _ssm_readout(h_re, h_im, cre_ref, cim_ref):
    y2 = _dot(h_re.astype(BF16), cre_ref[...]) - _dot(h_im.astype(BF16), cim_ref[...])
    row = lax.broadcasted_iota(jnp.int32, y2.shape, 0)
    col = lax.broadcasted_iota(jnp.int32, y2.shape, 1)
    keep = (row & (SUBLANES - 1)) == (col >> 5)
    y2 = jnp.where(keep, y2, 0.0)
    n = y2.shape[0] // SUBLANES
    return jnp.sum(y2.reshape(n, SUBLANES, y2.shape[1]), axis=1)


def _ssm_glu(y, u, d_ref, wglu_ref, bglu_ref):
    y = y + d_ref[...] * u
    g = _gelu_tanh(y)
    return g * jax.nn.sigmoid(_dot(g.astype(BF16), wglu_ref[...]) + bglu_ref[...])


def _rope_rows(x, cos_t, sin_t):
    half = HEAD_DIM // 2
    parts = []
    for h0 in range(0, x.shape[0], HEAD_DIM):
        parts += [x[h0 + half:h0 + HEAD_DIM], x[h0:h0 + half]]
    return x * cos_t + jnp.concatenate(parts, axis=0) * sin_t


def _inproj_kernel(x_ref, gpre_ref, w_ref, wkt_ref, wvt_ref, cos_ref, sin_ref, cost_ref, sint_ref,
                   lng_ref, lnb_ref, ws_ref, bs_ref, *rest):
    u_ref, yb_ref, q_ref, kt_ref, v_ref, kbf_ref, vt_ref, gs_ref, km_ref = rest[-9:]
    tm = x_ref.shape[0]
    h = _rmsnorm(x_ref[...], gpre_ref[...]).astype(BF16)

    def seg(a, b):
        return _dot(h, w_ref[:, a:b])

    u_ref[...] = seg(0, 256)

    u_sgu = seg(256, 512)
    vn = _layernorm(seg(512, 768), lng_ref[...], lnb_ref[...])
    r = lax.broadcasted_iota(jnp.int32, (CHUNK, CHUNK), 0)
    c = lax.broadcasted_iota(jnp.int32, (CHUNK, CHUNK), 1)
    tril = c <= r
    grp = lax.broadcasted_iota(jnp.int32, (CHUNK, 256), 1) >> 6
    wmask = [jnp.where(tril, ws_ref[g], 0.0).astype(BF16) for g in range(SGU_GROUPS)]
    for ci in range(tm // CHUNK):
        rows = slice(ci * CHUNK, (ci + 1) * CHUNK)
        vc = vn[rows]
        mixed = bs_ref[...]
        for g in range(SGU_GROUPS):
            mixed = mixed + _dot(wmask[g], jnp.where(grp == g, vc, 0.0).astype(BF16))
        yb_ref[rows, :] = (u_sgu[rows] * mixed).astype(BF16)

    cos = cos_ref[...]
    sin = sin_ref[...]
    q_ref[...] = _rope(seg(768, 1280), cos, sin)
    k = _rope(seg(1280, 1792), cos, sin)
    kbf_ref[...] = k.astype(BF16)
    km_ref[0] = jnp.mean(k, axis=0, keepdims=True)
    kt_ref[...] = _rope_rows(_dot_nt(wkt_ref[...], h), cost_ref[...], sint_ref[...])
    v_t = _dot_nt(wvt_ref[...], h)
    v_ref[...] = v_t
    vt_ref[...] = v_t.astype(BF16)
    for j in range(3):
        gs_ref[:, j * 1024:(j + 1) * 1024] = jax.nn.sigmoid(
            seg(2304 + j * 1024, 2304 + (j + 1) * 1024)).astype(BF16)


def _inproj(x2d, l, depth, w_in_bf, wkt_bf, wvt_bf, gpre, cos_t, sin_t, cos_tt, sin_tt, lng, lnb, sgu_w, bs2d,
            seq_len, kv_stacks):
    m, d = x2d.shape
    tm = MOBA_BLOCK
    n_t = seq_len // tm
    n_seq = m // seq_len
    in_w = w_in_bf.shape[-1]
    row = lambda i: (i, 0)
    lyr3 = lambda i: (l, 0, 0)
    stack = jax.ShapeDtypeStruct((depth, n_seq, 512, seq_len), F32)
    stack_spec = pl.BlockSpec((None, None, 512, tm), lambda i: (l, i // n_t, 0, i % n_t))
    outs = (
        jax.ShapeDtypeStruct((m, 256), F32),
        jax.ShapeDtypeStruct((m, 256), BF16),
        jax.ShapeDtypeStruct((m, 512), F32),
        stack,
        stack,
        jax.ShapeDtypeStruct((m, 512), BF16),
        jax.ShapeDtypeStruct((512, m), BF16),
        jax.ShapeDtypeStruct((m, 3072), BF16),
        jax.ShapeDtypeStruct((m // tm, 1, 512), F32),
    )
    in_specs = [
        pl.BlockSpec((tm, d), row),
        pl.BlockSpec((None, 1, d), lyr3),
        _resident((None, d, in_w), lyr3),
        _resident((None, 512, d), lyr3),
        _resident((None, 512, d), lyr3),
        pl.BlockSpec((tm, 512), lambda i: (i % n_t, 0)),
        pl.BlockSpec((tm, 512), lambda i: (i % n_t, 0)),
        pl.BlockSpec((512, tm), lambda i: (0, i % n_t)),
        pl.BlockSpec((512, tm), lambda i: (0, i % n_t)),
        pl.BlockSpec((None, 1, 256), lyr3),
        pl.BlockSpec((None, 1, 256), lyr3),
        pl.BlockSpec((None, SGU_GROUPS, CHUNK, CHUNK), lambda i: (l, 0, 0, 0)),
        pl.BlockSpec((None, CHUNK, 256), lyr3),
    ]
    args = [x2d, gpre, w_in_bf, wkt_bf, wvt_bf, cos_t, sin_t, cos_tt, sin_tt, lng, lnb, sgu_w, bs2d]
    aliases = {len(args): 3, len(args) + 1: 4}
    in_specs += [pl.BlockSpec(memory_space=pl.ANY)] * 2
    args += list(kv_stacks)
    return pl.pallas_call(
        _inproj_kernel,
        grid=(m // tm,),
        in_specs=in_specs,
        out_specs=[
            pl.BlockSpec((tm, 256), row), pl.BlockSpec((tm, 256), row),
            pl.BlockSpec((tm, 512), row), stack_spec, stack_spec,
            pl.BlockSpec((tm, 512), row), pl.BlockSpec((512, tm), lambda i: (0, i)),
            pl.BlockSpec((tm, 3072), row),
            pl.BlockSpec((1, 1, 512), lambda i: (i, 0, 0)),
        ],
        out_shape=outs,
        input_output_aliases=aliases,
        compiler_params=_cparams(("arbitrary",)),
        name="inproj",
    )(*args)


def _ssm_kernel(u_ref, bcat_ref, ccat_ref, ar_ref, ai_ref,
                d_ref, wglu_ref, bglu_ref, ya_ref, hre_ref, him_ref, sre, sim):
    nb, tt = u_ref.shape[0], u_ref.shape[1]
    step = pl.program_id(0)

    @pl.when(step == 0)
    def _():
        hre_ref[...] = jnp.zeros_like(hre_ref)
        him_ref[...] = jnp.zeros_like(him_ref)

    for b in range(nb):
        ub = u_ref[b].astype(BF16)
        for j in range(SUBLANES):
            rows_j = pl.ds(j, tt, stride=SUBLANES)
            bu = _dot(ub, bcat_ref[j])
            sre[b, rows_j, :] = bu[:, :LANES]
            sim[b, rows_j, :] = bu[:, LANES:]

    ar = ar_ref[...]
    ai = ai_ref[...]

    def body(t, carry):
        off = pl.multiple_of(t * SUBLANES, SUBLANES)
        new = []
        for b in range(nb):
            hr, hi = carry[2 * b], carry[2 * b + 1]
            nr = ar * hr - ai * hi + sre[b, pl.ds(off, SUBLANES), :]
            ni = ar * hi + ai * hr + sim[b, pl.ds(off, SUBLANES), :]
            sre[b, pl.ds(off, SUBLANES), :] = nr
            sim[b, pl.ds(off, SUBLANES), :] = ni
            new += [nr, ni]
        return tuple(new)

    init = []
    for b in range(nb):
        init += [hre_ref[b], him_ref[b]]
    fin = lax.fori_loop(0, tt, body, tuple(init), unroll=8)
    for b in range(nb):
        hre_ref[b] = fin[2 * b]
        him_ref[b] = fin[2 * b + 1]

    for b in range(nb):
        y = jnp.zeros((tt, u_ref.shape[2]), F32)
        for j in range(SUBLANES):
            rows_j = pl.ds(j, tt, stride=SUBLANES)
            h_j = jnp.concatenate([sre[b, rows_j, :], sim[b, rows_j, :]], axis=1).astype(BF16)
            y = y + _dot(h_j, ccat_ref[j])
        ya_ref[b] = _ssm_glu(y, u_ref[b], d_ref, wglu_ref, bglu_ref).astype(BF16)


def _ssm_scan(u3, l, tt, bcat8, ccat8, ar, ai, d_skip, wglu_bf, bglu):
    nb, t, w = u3.shape
    lyr3 = lambda i: (l, 0, 0)
    lyr4 = lambda i: (l, 0, 0, 0)
    return pl.pallas_call(
        _ssm_kernel,
        grid=(t // tt,),
        in_specs=[
            pl.BlockSpec((nb, tt, w), lambda i: (0, i, 0)),
            pl.BlockSpec((None, SUBLANES, 256, 256), lyr4), pl.BlockSpec((None, SUBLANES, 256, 256), lyr4),
            pl.BlockSpec((None, SUBLANES, LANES), lyr3), pl.BlockSpec((None, SUBLANES, LANES), lyr3),
            pl.BlockSpec((None, 1, 256), lyr3),
            pl.BlockSpec((None, 256, 256), lyr3),
            pl.BlockSpec((None, 1, 256), lyr3),
        ],
        out_specs=[
            pl.BlockSpec((nb, tt, w), lambda i: (0, i, 0)),
            pl.BlockSpec((nb, SUBLANES, LANES), lambda i: (0, 0, 0)),
            pl.BlockSpec((nb, SUBLANES, LANES), lambda i: (0, 0, 0)),
        ],
        out_shape=(
            jax.ShapeDtypeStruct((nb, t, w), BF16),
            jax.ShapeDtypeStruct((nb, SUBLANES, LANES), F32),
            jax.ShapeDtypeStruct((nb, SUBLANES, LANES), F32),
        ),
        scratch_shapes=[pltpu.VMEM((nb, SUBLANES * tt, LANES), F32),
                        pltpu.VMEM((nb, SUBLANES * tt, LANES), F32)],
        compiler_params=_cparams(("arbitrary",)),
        name="ssm_scan",
    )(u3, bcat8, ccat8, ar, ai, d_skip, wglu_bf, bglu)


def _top3_rows(gate, blk, own):
    big = jnp.int32(1 << 20)
    g = jnp.where(blk < own, gate, NEG)
    sel = jnp.zeros(gate.shape, jnp.bool_)
    for _ in range(MOBA_TOPK):
        m = jnp.max(g, axis=0, keepdims=True)
        first = jnp.min(jnp.where(g == m, blk, big), axis=0, keepdims=True)
        pick = blk == first
        sel = sel | (pick & (blk < own))
        g = jnp.where(pick, -jnp.inf, g)
    return sel


def _split_bf16(x):
    hi = x.astype(BF16)
    return hi, (x - hi.astype(F32)).astype(BF16)


def _attn_kernel(q_ref, k_ref, vt_ref, km_ref, o_ref, kaug, vaug, qaug, acc, sbuf0, sbuf1):
    qt = pl.program_id(2)
    tq = q_ref.shape[1]
    tk = tq
    t = k_ref.shape[1]
    nblk = km_ref.shape[1]
    shift = MOBA_BLOCK.bit_length() - 1

    @pl.when(qt == 0)
    def _():
        def fill(i, carry):
            off = pl.multiple_of(i * tk, tk)
            kk = k_ref[0, pl.ds(off, tk), :].astype(F32)
            lane = lax.broadcasted_iota(jnp.int32, kk.shape, 1)
            blk = (lax.broadcasted_iota(jnp.int32, kk.shape, 0) + off) >> shift
            hot_a = jnp.where(lane - HEAD_DIM == blk, 1.0, 0.0)
            hot_b = jnp.where(lane == blk, 1.0, 0.0)
            kaug[0, pl.ds(off, tk), :] = jnp.where(lane < HEAD_DIM, kk, hot_a).astype(BF16)
            kaug[1, pl.ds(off, tk), :] = jnp.where(lane >= HEAD_DIM, kk, hot_b).astype(BF16)
            ones = jnp.ones((VAUG_ROWS - HEAD_DIM, tk), BF16)
            for e in range(2):
                v_e = vt_ref[e * HEAD_DIM:(e + 1) * HEAD_DIM, pl.ds(off, tk)]
                vaug[e, :, pl.ds(off, tk)] = jnp.concatenate([v_e, ones], axis=0)
            return carry
        lax.fori_loop(0, t // tk, fill, 0)

    q_t = q_ref[0].T
    km = km_ref[0]
    km_lane = lax.broadcasted_iota(jnp.int32, km.shape, 1)
    blk = lax.broadcasted_iota(jnp.int32, (nblk, tq), 0)
    col = lax.broadcasted_iota(jnp.int32, (nblk, tq), 1)
    own = (qt * tq + col) >> shift
    q_hi, q_lo = _split_bf16(q_t)
    qs = q_t * (HEAD_DIM ** -0.5 * math.log2(math.e))
    pad = jnp.zeros((HEAD_DIM - nblk, tq), F32)
    for e in range(2):
        k_hi, k_lo = _split_bf16(jnp.where((km_lane >> 6) == e, km, 0.0))
        gate = _dot(k_hi, q_hi) + _dot(k_lo, q_hi) + _dot(k_hi, q_lo)
        sel = _top3_rows(gate, blk, own)
        bias = jnp.where(sel | (blk == own), 0.0, NEG)
        if e == 0:
            qa = jnp.concatenate([qs[0:HEAD_DIM], bias, pad], axis=0)
        else:
            qa = jnp.concatenate([bias, pad, qs[HEAD_DIM:2 * HEAD_DIM]], axis=0)
        qaug[e] = qa.astype(BF16)

    acc[...] = jnp.zeros_like(acc)

    def scores(j, dst):
        off = pl.multiple_of(j * tk, tk)
        for e in range(2):
            dst[e] = _dot(kaug[e, pl.ds(off, tk), :], qaug[e])

    def step(j, carry, cur, nxt, last=False):
        if not last:
            scores(j + 1, nxt)
        off = pl.multiple_of(j * tk, tk)
        new = []
        for e in range(2):
            m = carry[e]
            s = cur[e]
            if last:
                r = lax.broadcasted_iota(jnp.int32, s.shape, 0)
                c = lax.broadcasted_iota(jnp.int32, s.shape, 1)
                s = jnp.where(r <= c, s, NEG)
            m_new = jnp.maximum(m, jnp.max(s, axis=0, keepdims=True))
            alpha = jnp.exp2(m - m_new)
            p = jnp.exp2(s - m_new).astype(BF16)
            acc[e] = alpha * acc[e] + _dot(vaug[e, :, pl.ds(off, tk)], p)
            new += [m_new]
        return tuple(new)

    m0 = jnp.full((1, tq), -1e38, F32)
    scores(0, sbuf0)

    def pair(i, carry):
        carry = step(2 * i, carry, sbuf0, sbuf1)
        return step(2 * i + 1, carry, sbuf1, sbuf0)

    carry = lax.fori_loop(0, qt // 2, pair, (m0, m0))

    def tail_even(carry):
        return step(qt, carry, sbuf0, sbuf1, last=True)

    def tail_odd(carry):
        carry = step(qt - 1, carry, sbuf0, sbuf1)
        return step(qt, carry, sbuf1, sbuf0, last=True)

    lax.cond(qt % 2 == 0, tail_even, tail_odd, carry)
    o_t = jnp.concatenate([acc[e, 0:HEAD_DIM] / acc[e, HEAD_DIM:HEAD_DIM + 1] for e in range(2)], axis=0)
    o_ref[0] = o_t.T.astype(BF16)


def _attention(q3, kbf3, v_t, km3):
    nb, t, w = q3.shape
    nblk = km3.shape[1]
    tq = min(ATT_TILE, t)
    assert nblk <= HEAD_DIM and t % tq == 0
    return pl.pallas_call(
        _attn_kernel,
        grid=(nb, w // LANES, t // tq),
        in_specs=[
            pl.BlockSpec((1, tq, LANES), lambda b, hp, i: (b, i, hp)),
            pl.BlockSpec((1, t, LANES), lambda b, hp, i: (b, 0, hp)),
            pl.BlockSpec((LANES, t), lambda b, hp, i: (hp, b)),
            pl.BlockSpec((1, nblk, LANES), lambda b, hp, i: (b, 0, hp)),
        ],
        out_specs=pl.BlockSpec((1, tq, LANES), lambda b, hp, i: (b, i, hp)),
        out_shape=jax.ShapeDtypeStruct((nb, t, w), BF16),
        scratch_shapes=[pltpu.VMEM((2, t, LANES), BF16),
                        pltpu.VMEM((2, VAUG_ROWS, t), BF16),
                        pltpu.VMEM((2, LANES, tq), BF16),
                        pltpu.VMEM((2, VAUG_ROWS, tq), F32),
                        pltpu.VMEM((2, tq, tq), F32),
                        pltpu.VMEM((2, tq, tq), F32)],
        compiler_params=_cparams(("arbitrary", "arbitrary", "arbitrary")),
        name="moba_attention",
    )(q3, kbf3, v_t, km3)


def _merge_kernel(x_ref, ya_ref, yb_ref, yc_ref, gs_ref, wa_ref, wb_ref, wc_ref, wo_ref, gpost_ref, o_ref):
    d = x_ref.shape[1]
    merged = (gs_ref[:, 0:d].astype(F32) * _dot(ya_ref[...], wa_ref[...])
              + gs_ref[:, d:2 * d].astype(F32) * _dot(yb_ref[...], wb_ref[...])
              + gs_ref[:, 2 * d:3 * d].astype(F32) * _dot(yc_ref[...], wc_ref[...]))
    mix = _dot(merged.astype(BF16), wo_ref[...])
    o_ref[...] = x_ref[...] + _rmsnorm(mix, gpost_ref[...])


def _merge(x2d, ya, yb, yc, gs, l, wa, wb, wc, wo, gpost, tm):
    m, d = x2d.shape
    row = lambda i: (i, 0)
    lyr = lambda i: (l, 0, 0)
    return pl.pallas_call(
        _merge_kernel,
        grid=(m // tm,),
        in_specs=[
            pl.BlockSpec((tm, d), row),
            pl.BlockSpec((tm, 256), row), pl.BlockSpec((tm, 256), row), pl.BlockSpec((tm, 512), row),
            pl.BlockSpec((tm, 3 * d), row),
            _resident((None, 256, d), lyr), _resident((None, 256, d), lyr),
            _resident((None, 512, d), lyr), _resident((None, d, d), lyr),
            pl.BlockSpec((None, 1, d), lyr),
        ],
        out_specs=pl.BlockSpec((tm, d), row),
        out_shape=jax.ShapeDtypeStruct((m, d), F32),
        compiler_params=_cparams(("arbitrary",)),
        name="merge",
    )(x2d, ya, yb, yc, gs, wa, wb, wc, wo, gpost)


def _ffn_kernel(x_ref, gpre_ref, wup_ref, wdn_ref, gpost_ref, o_ref):
    x = x_ref[...]
    h = _rmsnorm(x, gpre_ref[...]).astype(BF16)
    d = x.shape[1]
    dff = wup_ref.shape[1]
    acc = jnp.zeros(x.shape, F32)
    for j in range(dff // d):
        a = jnp.maximum(_dot(h, wup_ref[:, j * d:(j + 1) * d]), 0.0)
        acc = acc + _dot((a * a).astype(BF16), wdn_ref[j * d:(j + 1) * d, :])
    o_ref[...] = x + _rmsnorm(acc, gpost_ref[...])


def _ffn(x2d, l, gpre, wup, wdn, gpost, tm):
    m, d = x2d.shape
    dff = wup.shape[-1]
    row = lambda i: (i, 0)
    lyr = lambda i: (l, 0, 0)
    return pl.pallas_call(
        _ffn_kernel,
        grid=(m // tm,),
        in_specs=[
            pl.BlockSpec((tm, d), row),
            pl.BlockSpec((None, 1, d), lyr),
            _resident((None, d, dff), lyr), _resident((None, dff, d), lyr),
            pl.BlockSpec((None, 1, d), lyr),
        ],
        out_specs=pl.BlockSpec((tm, d), row),
        out_shape=jax.ShapeDtypeStruct((m, d), F32),
        compiler_params=_cparams(("arbitrary",)),
        name="ffn",
    )(x2d, gpre, wup, wdn, gpost)


def _sample_pre_kernel(x_ref, gpre_ref, w_ref, cos_ref, sin_ref, lng_ref, lnb_ref, w00_ref, b0_ref,
                       rep_ref, bre_ref, bim_ref, cre_ref, cim_ref, ar_ref, ai_ref, h0r_ref, h0i_ref,
                       d_ref, wglu_ref, bglu_ref,
                       ya_ref, yb_ref, q_ref, k_ref, v_ref, gs_ref, hr_ref, hi_ref, vrow_ref):
    h = _rmsnorm(x_ref[...], gpre_ref[...]).astype(BF16)

    def seg(a, b):
        return _dot(h, w_ref[:, a:b])

    u = seg(0, 256)
    bu_re, bu_im = _ssm_expand(u, rep_ref, bre_ref, bim_ref)
    ar, ai = ar_ref[...], ai_ref[...]
    h0r, h0i = h0r_ref[...], h0i_ref[...]
    hr = ar * h0r - ai * h0i + bu_re
    hi = ar * h0i + ai * h0r + bu_im
    hr_ref[...] = hr
    hi_ref[...] = hi
    y = _ssm_readout(hr, hi, cre_ref, cim_ref)
    ya_ref[...] = _ssm_glu(y, u, d_ref, wglu_ref, bglu_ref).astype(BF16)

    vn = _layernorm(seg(512, 768), lng_ref[...], lnb_ref[...])
    vrow_ref[...] = vn
    yb_ref[...] = (seg(256, 512) * (w00_ref[...] * vn + b0_ref[...])).astype(BF16)

    cos, sin = cos_ref[...], sin_ref[...]
    q_ref[...] = _rope(seg(768, 1280), cos, sin)
    k_ref[...] = _rope(seg(1280, 1792), cos, sin)
    v_ref[...] = seg(1792, 2304)
    for j in range(3):
        gs_ref[:, j * 1024:(j + 1) * 1024] = jax.nn.sigmoid(
            seg(2304 + j * 1024, 2304 + (j + 1) * 1024)).astype(BF16)


def _sample_pre(xs, l, w_in_bf, gpre, cos_s, sin_s, lng, lnb, w00, b0, rep_s, bre, bim, cre, cim,
                ar_s, ai_s, h0r, h0i, d_skip, wglu_bf, bglu):
    ns, d = xs.shape
    in_w = w_in_bf.shape[-1]
    c2 = lambda i: (0, 0)
    lyr = lambda i: (l, 0, 0)
    full = lambda a: pl.BlockSpec(a.shape, c2)
    outs = (
        jax.ShapeDtypeStruct((ns, 256), BF16), jax.ShapeDtypeStruct((ns, 256), BF16),
        jax.ShapeDtypeStruct((ns, 512), F32), jax.ShapeDtypeStruct((ns, 512), F32),
        jax.ShapeDtypeStruct((ns, 512), F32), jax.ShapeDtypeStruct((ns, 3072), BF16),
        jax.ShapeDtypeStruct((ns * SUBLANES, LANES), F32), jax.ShapeDtypeStruct((ns * SUBLANES, LANES), F32),
        jax.ShapeDtypeStruct((ns, 256), F32),
    )
    return pl.pallas_call(
        _sample_pre_kernel,
        grid=(1,),
        in_specs=[
            full(xs),
            pl.BlockSpec((None, 1, d), lyr),
            pl.BlockSpec((None, d, in_w), lyr),
            full(cos_s), full(sin_s),
            pl.BlockSpec((None, 1, 256), lyr), pl.BlockSpec((None, 1, 256), lyr),
            pl.BlockSpec((None, 1, 256), lyr), pl.BlockSpec((None, 1, 256), lyr),
            full(rep_s),
            pl.BlockSpec((None, 256, LANES), lyr), pl.BlockSpec((None, 256, LANES), lyr),
            pl.BlockSpec((None, LANES, 256), lyr), pl.BlockSpec((None, LANES, 256), lyr),
            pl.BlockSpec((None, ns * SUBLANES, LANES), lyr), pl.BlockSpec((None, ns * SUBLANES, LANES), lyr),
            pl.BlockSpec((None, ns * SUBLANES, LANES), lyr), pl.BlockSpec((None, ns * SUBLANES, LANES), lyr),
            pl.BlockSpec((None, 1, 256), lyr),
            pl.BlockSpec((None, 256, 256), lyr),
            pl.BlockSpec((None, 1, 256), lyr),
        ],
        out_specs=[pl.BlockSpec(o.shape, c2) for o in outs],
        out_shape=outs,
        compiler_params=_cparams(("arbitrary",)),
        name="sample_pre",
    )(xs, gpre, w_in_bf, cos_s, sin_s, lng, lnb, w00, b0, rep_s, bre, bim, cre, cim,
      ar_s, ai_s, h0r, h0i, d_skip, wglu_bf, bglu)


def _cache_select_kernel(pt_ref, q_ref, ck_ref, sel_ref, kmt, pbuf, sem, *, layer, nblk):
    b, j = pl.program_id(0), pl.program_id(1)
    n_j = pl.num_programs(1)
    step = b * n_j + j
    n_steps = pl.num_programs(0) * n_j
    slot = step % 2
    pp = pbuf.shape[1]
    per_blk = MOBA_BLOCK // PAGE_SIZE
    blk_per_step = pp // per_blk
    rows = kmt.shape[0]

    def copies(seq, chunk, sl):
        return [pltpu.make_async_copy(ck_ref.at[layer, pt_ref[seq, chunk * pp + i]], pbuf.at[sl, i], sem.at[sl])
                for i in range(pp)]

    @pl.when(step == 0)
    def _():
        for c in copies(0, 0, 0):
            c.start()

    @pl.when(step + 1 < n_steps)
    def _():
        wrap = j + 1 == n_j
        for c in copies(jnp.where(wrap, b + 1, b), jnp.where(wrap, 0, j + 1), 1 - slot):
            c.start()

    for c in copies(b, j, slot):
        c.wait()

    @pl.when(j == 0)
    def _():
        kmt[...] = jnp.zeros_like(kmt)

    col = lax.broadcasted_iota(jnp.int32, (PAGE_SIZE, LANES), 1)
    tot = jnp.zeros(kmt.shape, F32)
    for i in range(blk_per_step):
        x = pbuf[slot, per_blk * i].reshape(rows, PAGE_SIZE)
        for jj in range(1, per_blk):
            x = x + pbuf[slot, per_blk * i + jj].reshape(rows, PAGE_SIZE)
        pick = jnp.where(col == j * blk_per_step + i, 1.0, 0.0).astype(BF16)
        x_hi, x_lo = _split_bf16(x)
        tot = tot + _dot(x_hi, pick) + _dot(x_lo, pick)
    acc = kmt[...] + tot * (1.0 / MOBA_BLOCK)
    kmt[...] = acc

    @pl.when(j == pl.num_programs(1) - 1)
    def _():
        n_heads = sel_ref.shape[1]
        q = q_ref[0]
        hrow = lax.broadcasted_iota(jnp.int32, (n_heads, rows), 0)
        hlane = lax.broadcasted_iota(jnp.int32, (n_heads, rows), 1)
        qbd = jnp.where((hlane >> 6) == hrow, jnp.broadcast_to(q, (n_heads, rows)), 0.0)
        gate = jnp.dot(qbd, acc, preferred_element_type=F32, precision=lax.Precision.HIGHEST)
        idx = lax.broadcasted_iota(jnp.int32, gate.shape, 1)
        big = jnp.int32(1 << 20)
        g = jnp.where(idx < nblk, gate, -jnp.inf)
        out = jnp.zeros(gate.shape, jnp.int32)
        for r in range(MOBA_TOPK):
            m = jnp.max(g, axis=-1, keepdims=True)
            first = jnp.min(jnp.where(g == m, idx, big), axis=-1, keepdims=True)
            out = jnp.where(idx == r, first, out)
            g = jnp.where(idx == first, -jnp.inf, g)
        sel_ref[0] = out


def _cache_select(cache_kt, page_table, q_s, l):
    ns, n_pages = page_table.shape
    n_heads, hd, psz = cache_kt.shape[2:]
    w = n_heads * hd
    pp = min(PAGES_PER_STEP, n_pages)
    nblk = n_pages * psz // MOBA_BLOCK
    assert nblk <= LANES and nblk >= MOBA_TOPK

    grid_spec = pltpu.PrefetchScalarGridSpec(
        num_scalar_prefetch=1,
        grid=(ns, n_pages // pp),
        in_specs=[pl.BlockSpec((1, 1, w), lambda b, j, pt: (b, 0, 0)), pl.BlockSpec(memory_space=pl.ANY)],
        out_specs=pl.BlockSpec((1, n_heads, LANES), lambda b, j, pt: (b, 0, 0)),
        scratch_shapes=[pltpu.VMEM((w, LANES), F32),
                        pltpu.VMEM((2, pp, n_heads, hd, psz), F32),
                        pltpu.SemaphoreType.DMA((2,))],
    )
    return pl.pallas_call(
        functools.partial(_cache_select_kernel, layer=l, nblk=nblk),
        grid_spec=grid_spec,
        out_shape=jax.ShapeDtypeStruct((ns, n_heads, LANES), jnp.int32),
        compiler_params=_cparams(("arbitrary", "arbitrary")),
        name="cache_select",
    )(page_table, q_s.reshape(ns, 1, w), cache_kt)


def _sample_attn_kernel(pt_ref, sel_ref, q_ref, kn_ref, vn_ref, ck_ref, cv_ref, o_ref,
                        kbuf, vbuf, sem, *, layer, n_heads):
    b = pl.program_id(0)
    ns = pl.num_programs(0)
    per_blk = MOBA_BLOCK // PAGE_SIZE
    slot = b % 2

    def copies(seq, sl):
        out = []
        for h in range(n_heads):
            rows = pl.ds(h * HEAD_DIM, HEAD_DIM)
            for r in range(MOBA_TOPK):
                blk = sel_ref[(seq * n_heads + h) * MOBA_TOPK + r]
                for j in range(per_blk):
                    page = pt_ref[seq, blk * per_blk + j]
                    cols = pl.ds((r * per_blk + j) * PAGE_SIZE, PAGE_SIZE)
                    out.append(pltpu.make_async_copy(ck_ref.at[layer, page, h],
                                                     kbuf.at[sl, rows, cols], sem.at[0, sl]))
                    out.append(pltpu.make_async_copy(cv_ref.at[layer, page, h],
                                                     vbuf.at[sl, rows, cols], sem.at[1, sl]))
        return out

    @pl.when(b == 0)
    def _():
        for c in copies(0, 0):
            c.start()

    @pl.when(b + 1 < ns)
    def _():
        for c in copies(b + 1, 1 - slot):
            c.start()

    for c in copies(b, slot):
        c.wait()

    w = n_heads * HEAD_DIM
    kb = kbuf[slot].astype(BF16)
    vb = vbuf[slot].astype(BF16)
    q = q_ref[0] * (HEAD_DIM ** -0.5)
    hrow = lax.broadcasted_iota(jnp.int32, (n_heads, w), 0)
    hlane = lax.broadcasted_iota(jnp.int32, (n_heads, w), 1)
    mine = (hlane >> 6) == hrow
    qbd = jnp.where(mine, jnp.broadcast_to(q, (n_heads, w)), 0.0)
    s = _dot(qbd.astype(BF16), kb)
    s_new = jnp.sum(qbd * kn_ref[0], axis=-1, keepdims=True)
    m = jnp.maximum(jnp.max(s, axis=-1, keepdims=True), s_new)
    p = jnp.exp(s - m)
    p_new = jnp.exp(s_new - m)
    den = jnp.sum(p, axis=-1, keepdims=True) + p_new
    o_all = _dot_nt(p.astype(BF16), vb) + p_new * vn_ref[0]
    o = jnp.sum(jnp.where(mine, o_all / den, 0.0), axis=0, keepdims=True)
    o_ref[0] = o.astype(o_ref.dtype)


def _sample_attention(q_s, k_s, v_s, cache_kt, cache_vt, page_table, sel_flat, l):
    ns, w = q_s.shape
    n_heads = w // HEAD_DIM
    cols = MOBA_TOPK * MOBA_BLOCK
    q3, k3, v3 = (a.reshape(ns, 1, w) for a in (q_s, k_s, v_s))
    row = lambda b, pt, sel: (b, 0, 0)
    grid_spec = pltpu.PrefetchScalarGridSpec(
        num_scalar_prefetch=2,
        grid=(ns,),
        in_specs=[pl.BlockSpec((1, 1, w), row), pl.BlockSpec((1, 1, w), row), pl.BlockSpec((1, 1, w), row),
                  pl.BlockSpec(memory_space=pl.ANY), pl.BlockSpec(memory_space=pl.ANY)],
        out_specs=pl.BlockSpec((1, 1, w), row),
        scratch_shapes=[pltpu.VMEM((2, w, cols), F32),
                        pltpu.VMEM((2, w, cols), F32),
                        pltpu.SemaphoreType.DMA((2, 2))],
    )
    out = pl.pallas_call(
        functools.partial(_sample_attn_kernel, layer=l, n_heads=n_heads),
        grid_spec=grid_spec,
        out_shape=jax.ShapeDtypeStruct((ns, 1, w), BF16),
        compiler_params=_cparams(("arbitrary",)),
        name="sample_attention",
    )(page_table, sel_flat, q3, k3, v3, cache_kt, cache_vt)
    return out.reshape(ns, w)


def _rope_tables(pos, n_heads):
    half = HEAD_DIM // 2
    inv_freq = ROPE_THETA ** (-jnp.arange(half, dtype=F32) / half)
    ang = pos.astype(F32)[:, None] * inv_freq[None, :]
    cos, sin = jnp.cos(ang), jnp.sin(ang)
    cos_t = jnp.tile(jnp.concatenate([cos, cos], axis=-1), (1, n_heads))
    sin_t = jnp.tile(jnp.concatenate([-sin, sin], axis=-1), (1, n_heads))
    return cos_t, sin_t


def _ssm_tables(a_re, a_im, log_dt, b_re, b_im, c_re, c_im):
    depth, groups, p = a_re.shape
    lam_re = jnp.minimum(a_re.astype(F32), DT_EPS_LAM)
    lam_im = a_im.astype(F32)
    dt = jnp.exp(log_dt.astype(F32))[..., None]
    mag = jnp.exp(lam_re * dt)
    ab_re = mag * jnp.cos(lam_im * dt)
    ab_im = mag * jnp.sin(lam_im * dt)
    den = lam_re * lam_re + lam_im * lam_im
    num_re = ab_re - 1.0
    f_re = (num_re * lam_re + ab_im * lam_im) / den
    f_im = (ab_im * lam_re - num_re * lam_im) / den
    br, bi = b_re.astype(F32), b_im.astype(F32)
    bb_re = f_re[..., None] * br - f_im[..., None] * bi
    bb_im = f_re[..., None] * bi + f_im[..., None] * br
    eye2 = jnp.eye(2, dtype=F32)

    def b_stack(bb):
        t = bb.transpose(0, 1, 3, 2).reshape(depth, groups // 2, 2, SSM_GROUP, p)
        t = t[:, :, :, :, None, :] * eye2[None, None, :, None, :, None]
        return t.reshape(depth, groups * SSM_GROUP, 2 * p).astype(BF16)

    def c_stack(c):
        t = c.astype(F32).reshape(depth, groups // 2, 2, SSM_GROUP, p)
        t = t[:, :, :, :, None, :] * eye2[None, None, :, None, :, None]
        t = t.transpose(0, 4, 5, 1, 2, 3)
        return t.reshape(depth, 2 * p, groups * SSM_GROUP).astype(BF16)

    lay = lambda a: a.reshape(depth, SUBLANES, LANES)
    return (lay(ab_re), lay(ab_im), b_stack(bb_re), b_stack(bb_im), c_stack(c_re), c_stack(c_im))


def _repeat_matrix(n):
    return jnp.repeat(jnp.eye(n, dtype=BF16), SUBLANES, axis=0)


def kernel(x_prompt, x_sample, cache_k, cache_v, state_ssm_re, state_ssm_im, page_table, mix_pre_g, w_in, ssm_a_re, ssm_a_im, ssm_log_dt, ssm_b_re, ssm_b_im, ssm_c_re, ssm_c_im, ssm_d, ssm_w_glu, ssm_b_glu, sgu_ln_g, sgu_ln_b, sgu_w, sgu_b, w_out_a, w_out_b, w_out_c, w_o, mix_post_g, ffn_pre_g, w_up, w_down, ffn_post_g):
    bp, tp, d = x_prompt.shape
    bs, ts, _ = x_sample.shape
    depth = w_in.shape[0]
    n_heads = cache_k.shape[3]
    n_pages = page_table.shape[1]
    past_len = n_pages * PAGE_SIZE
    groups = ssm_a_re.shape[1]
    assert ts == 1 and tp % MOBA_BLOCK == 0 and past_len % MOBA_BLOCK == 0
    assert n_pages % min(PAGES_PER_STEP, n_pages) == 0 and groups * SSM_STATE == SUBLANES * LANES

    row3 = lambda a: a.reshape(depth, 1, a.shape[-1]).astype(F32)
    w_in_bf = w_in.astype(BF16)
    wa_bf, wb_bf, wc_bf, wo_bf = (w.astype(BF16) for w in (w_out_a, w_out_b, w_out_c, w_o))
    wup_bf, wdn_bf = w_up.astype(BF16), w_down.astype(BF16)
    wglu_bf = ssm_w_glu.astype(BF16)
    gpre, gpost, fpre, fpost = row3(mix_pre_g), row3(mix_post_g), row3(ffn_pre_g), row3(ffn_post_g)
    lng, lnb, d_skip, bglu = row3(sgu_ln_g), row3(sgu_ln_b), row3(ssm_d), row3(ssm_b_glu)
    cos_p, sin_p = _rope_tables(jnp.arange(tp, dtype=jnp.int32), n_heads)
    cos_s, sin_s = _rope_tables(jnp.full((bs,), past_len, dtype=jnp.int32), n_heads)
    ar, ai, bre, bim, cre, cim = _ssm_tables(ssm_a_re, ssm_a_im, ssm_log_dt, ssm_b_re, ssm_b_im,
                                             ssm_c_re, ssm_c_im)
    sgu_dim = sgu_ln_g.shape[-1] // SGU_GROUPS
    bs2d = jnp.repeat(sgu_b.astype(F32).transpose(0, 2, 1), sgu_dim, axis=-1)
    w00 = jnp.repeat(sgu_w[:, :, 0, 0].astype(F32), sgu_dim, axis=-1).reshape(depth, 1, -1)
    b0 = bs2d[:, 0:1, :]
    cls = (jnp.arange(groups * SSM_GROUP) // (2 * SSM_GROUP))[None, :] == jnp.arange(SUBLANES)[:, None]
    bcat8 = jnp.where(cls[None, :, :, None], jnp.concatenate([bre, bim], axis=-1)[:, None], 0).astype(BF16)
    ccat8 = jnp.where(cls[None, :, None, :], jnp.concatenate([cre, -cim], axis=-2)[:, None], 0).astype(BF16)
    rep_s = _repeat_matrix(bs)
    ar_s, ai_s = jnp.tile(ar, (1, bs, 1)), jnp.tile(ai, (1, bs, 1))
    h0r = state_ssm_re.astype(F32).reshape(depth, bs * SUBLANES, LANES)
    h0i = state_ssm_im.astype(F32).reshape(depth, bs * SUBLANES, LANES)
    aw = n_heads * HEAD_DIM
    k0 = 256 + 2 * 256 + aw
    wkt_bf = w_in_bf[:, :, k0:k0 + aw].transpose(0, 2, 1)
    wvt_bf = w_in_bf[:, :, k0 + aw:k0 + 2 * aw].transpose(0, 2, 1)
    cos_pt, sin_pt = cos_p.T, sin_p.T
    cache_kt = cache_k.transpose(0, 1, 3, 4, 2)
    cache_vt = cache_v.transpose(0, 1, 3, 4, 2)

    xp = x_prompt.reshape(bp * tp, d)
    xs = x_sample.reshape(bs * ts, d)
    kv_stacks = (jnp.zeros((depth, bp, aw, tp), F32), jnp.zeros((depth, bp, aw, tp), F32))
    hrp_l, hip_l = [], []
    ks_l, vs_l, hrs_l, his_l, sv_l = [], [], [], [], []
    for l in range(depth):
        u, yb, q, kt_all, vt_all, kbf, v_t, gs, km = _inproj(
            xp, l, depth, w_in_bf, wkt_bf, wvt_bf, gpre, cos_p, sin_p, cos_pt, sin_pt, lng, lnb,
            sgu_w, bs2d, tp, kv_stacks)
        kv_stacks = (kt_all, vt_all)
        ya, hre, him = _ssm_scan(u.reshape(bp, tp, -1), l, SSM_TILE, bcat8, ccat8, ar, ai,
                                 d_skip, wglu_bf, bglu)
        yc = _attention(q.reshape(bp, tp, -1), kbf.reshape(bp, tp, -1), v_t,
                        km.reshape(bp, tp // MOBA_BLOCK, -1))
        xp = _merge(xp, ya.reshape(bp * tp, -1), yb, yc.reshape(bp * tp, -1), gs, l,
                    wa_bf, wb_bf, wc_bf, wo_bf, gpost, 512)
        xp = _ffn(xp, l, fpre, wup_bf, wdn_bf, fpost, 512)
        hrp_l.append(hre.reshape(bp, groups, SSM_STATE))
        hip_l.append(him.reshape(bp, groups, SSM_STATE))

        (ya_s, yb_s, q_s, k_s, v_s, gs_s, hr_s, hi_s, vrow) = _sample_pre(
            xs, l, w_in_bf, gpre, cos_s, sin_s, lng, lnb, w00, b0, rep_s, bre, bim, cre, cim,
            ar_s, ai_s, h0r, h0i, d_skip, wglu_bf, bglu)
        sel_s = _cache_select(cache_kt, page_table, q_s, l)
        sel_flat = sel_s[:, :, :MOBA_TOPK].reshape(-1)
        yc_s = _sample_attention(q_s, k_s, v_s, cache_kt, cache_vt, page_table, sel_flat, l)
        xs = _merge(xs, ya_s, yb_s, yc_s, gs_s, l, wa_bf, wb_bf, wc_bf, wo_bf, gpost, bs)
        xs = _ffn(xs, l, fpre, wup_bf, wdn_bf, fpost, bs)
        ks_l.append(k_s.reshape(bs, ts, n_heads, HEAD_DIM))
        vs_l.append(v_s.reshape(bs, ts, n_heads, HEAD_DIM))
        hrs_l.append(hr_s.reshape(bs, groups, SSM_STATE))
        his_l.append(hi_s.reshape(bs, groups, SSM_STATE))
        sv_l.append(vrow.reshape(bs, ts, -1))

    def rows_minor(stack):
        return stack.reshape(depth, bp, n_heads, HEAD_DIM, tp).transpose(0, 1, 4, 2, 3)

    return (xp.reshape(bp, tp, d), xs.reshape(bs, ts, d),
            rows_minor(kv_stacks[0]), rows_minor(kv_stacks[1]), jnp.stack(hrp_l), jnp.stack(hip_l),
            jnp.stack(ks_l), jnp.stack(vs_l), jnp.stack(hrs_l), jnp.stack(his_l), jnp.stack(sv_l))
```
